```python
import jax, jax.numpy as jnp
from jax import lax
import numpy as np

D_MODEL = 1024
BATCH = 8
SEQ = 4096
DEPTH = 2
DEC_BATCH = 2
DEC_SEQ = 16384
PAST_LEN = 128

MIX_WIDTH = D_MODEL
HG_HEADS = 4
HG_KEY = 128
HG_VAL = 128
HG_QK = HG_HEADS * HG_KEY
HG_V = HG_HEADS * HG_VAL
HG_CHUNK = 64
SG_HEADS = 4
SG_HEAD_DIM = (MIX_WIDTH - HG_V) // SG_HEADS
SG_WIDTH = SG_HEADS * SG_HEAD_DIM
SG_CHUNK = 128
IN_COLS = 3 * HG_QK + 2 * HG_V + 2 * SG_WIDTH
D_FF = 2816
CONV_W = 3
EPS = 1e-6

kernel_name = "hybrid_hgrn2_spatial_gating_encoder"


def _rmsnorm(x, g):
    xf = x.astype(jnp.float32)
    y = xf * lax.rsqrt(jnp.mean(xf * xf, axis=-1, keepdims=True) + EPS)
    return (y * g.astype(jnp.float32)).astype(x.dtype)


def _layernorm(x, g, b):
    xf = x.astype(jnp.float32)
    mu = jnp.mean(xf, axis=-1, keepdims=True)
    xc = xf - mu
    y = xc * lax.rsqrt(jnp.mean(xc * xc, axis=-1, keepdims=True) + EPS)
    return (y * g.astype(jnp.float32) + b.astype(jnp.float32)).astype(x.dtype)


def _hgrn2_direction(q, log_f, v):
    B, S, H, K = q.shape
    V = v.shape[-1]
    C = HG_CHUNK
    N = S // C
    k = -jnp.expm1(log_f)

    def to_chunks(a):
        return a.reshape(B, N, C, H, a.shape[-1]).transpose(1, 0, 3, 2, 4)

    qc, kc, vc = to_chunks(q), to_chunks(k), to_chunks(v)
    bc = jnp.cumsum(to_chunks(log_f), axis=3)
    lower = jnp.tril(jnp.ones((C, C), dtype=bool))[:, :, None]

    def step(state, inp):
        qq, kk, vv, bb = inp
        b_last = bb[:, :, -1:, :]
        o_inter = jnp.einsum('bhtk,bhkv->bhtv', qq * jnp.exp(bb), state)
        diff = bb[:, :, :, None, :] - bb[:, :, None, :, :]
        decay = jnp.where(lower, jnp.exp(jnp.minimum(diff, 0.0)), 0.0)
        scores = jnp.einsum('bhtk,bhsk,bhtsk->bhts', qq, kk, decay)
        o_intra = jnp.einsum('bhts,bhsv->bhtv', scores, vv)
        state = (jnp.exp(b_last[:, :, 0, :])[..., None] * state
                 + jnp.einsum('bhsk,bhsv->bhkv', kk * jnp.exp(b_last - bb), vv))
        return state, o_inter + o_intra

    s0 = jnp.zeros((B, H, K, V), jnp.float32)
    _, o = lax.scan(step, s0, (qc, kc, vc, bc))
    return o.transpose(1, 0, 3, 2, 4).reshape(B, S, H, V)


def _log_forget(z, lb):
    lb = lb.reshape(HG_HEADS, HG_KEY)
    return jnp.log(lb + (1.0 - lb) * jax.nn.sigmoid(z))


def _dwconv_centred(a, w, b):
    S = a.shape[1]
    ap = jnp.pad(a, ((0, 0), (1, 1), (0, 0)))
    return ap[:, :S] * w[0] + ap[:, 1:S + 1] * w[1] + ap[:, 2:] * w[2] + b


def _trunk(x, norm1_g, w_in, hg_lb_logits, hg_norm_g, sg_norm_g, sg_norm_b, sg_w_spatial,
           sg_b_spatial, w_out, norm2_g, w_ffn_in, ffn_conv_w, ffn_conv_b, w_ffn_out, final_norm_g):
    B, S, _ = x.shape
    dt = x.dtype
    p = jax.nn.softmax(hg_lb_logits.astype(jnp.float32), axis=1)
    lb_all = jnp.cumsum(p, axis=1) - p[:, :1]
    splits = np.cumsum([HG_QK, HG_QK, HG_QK, HG_V, HG_V, SG_WIDTH]).tolist()
    n_sg = S // SG_CHUNK
    for l in range(DEPTH):
        h = _rmsnorm(x, norm1_g[l])
        proj = h @ w_in[l]
        q, f_fw, f_bw, i_in, g, u, v = jnp.split(proj, splits, axis=-1)
        hk = lambda a: a.astype(jnp.float32).reshape(B, S, HG_HEADS, -1)
        qf, vf = hk(q), hk(i_in)
        lf_fw = _log_forget(hk(f_fw), lb_all[0, l])
        lf_bw = _log_forget(hk(f_bw), lb_all[1, l])
        o_fw = _hgrn2_direction(qf, lf_fw, vf)
        o_bw = jnp.flip(_hgrn2_direction(jnp.flip(qf, 1), jnp.flip(lf_bw, 1), jnp.flip(vf, 1)), 1)
        o = _rmsnorm(o_fw + o_bw, hg_norm_g[l].reshape(HG_HEADS, HG_VAL)).reshape(B, S, HG_V)
        hg_out = (o * jax.nn.silu(g.astype(jnp.float32))).astype(dt)
        u = jax.nn.gelu(u)
        vh = jax.nn.gelu(v).reshape(B, S, SG_HEADS, SG_HEAD_DIM)
        vh = _layernorm(vh, sg_norm_g[l].reshape(SG_HEADS, SG_HEAD_DIM), sg_norm_b[l].reshape(SG_HEADS, SG_HEAD_DIM))
        vc = vh.reshape(B, n_sg, SG_CHUNK, SG_HEADS, SG_HEAD_DIM)
        sp = jnp.einsum('hpq,bcqhd->bcphd', sg_w_spatial[l], vc) + sg_b_spatial[l].T[None, None, :, :, None]
        sg_out = u * sp.reshape(B, S, SG_WIDTH)
        x = x + (jnp.concatenate([hg_out, sg_out.astype(dt)], axis=-1) @ w_out[l]).astype(dt)
        h = _rmsnorm(x, norm2_g[l])
        a = _dwconv_centred(h @ w_ffn_in[l], ffn_conv_w[l], ffn_conv_b[l])
        gate, up = jnp.split(a, 2, axis=-1)
        x = x + ((jax.nn.silu(gate) * up) @ w_ffn_out[l]).astype(dt)
    return _rmsnorm(x, final_norm_g)


def setup_inputs(seed: int = 0) -> dict:
    key = jax.random.key(seed)
    ks = jax.random.split(key, 18)
    nrm = lambda k, shape, s: jax.random.normal(k, shape, jnp.float32) * s
    return {
        "x_prompt": nrm(ks[0], (BATCH, SEQ, D_MODEL), 1.0),
        "x_sample": nrm(ks[1], (DEC_BATCH, DEC_SEQ, D_MODEL), 1.0),
        "norm1_g": 1.0 + nrm(ks[2], (DEPTH, D_MODEL), 0.02),
        "w_in": nrm(ks[3], (DEPTH, D_MODEL, IN_COLS), D_MODEL ** -0.5),
        "hg_lb_logits": nrm(ks[4], (2, DEPTH, HG_QK), 1.0),
        "hg_norm_g": 1.0 + nrm(ks[5], (DEPTH, HG_V), 0.02),
        "sg_norm_g": 1.0 + nrm(ks[6], (DEPTH, SG_WIDTH), 0.02),
        "sg_norm_b": nrm(ks[7], (DEPTH, SG_WIDTH), 0.02),
        "sg_w_spatial": nrm(ks[8], (DEPTH, SG_HEADS, SG_CHUNK, SG_CHUNK), 0.5 * SG_CHUNK ** -0.5),
        "sg_b_spatial": 1.0 + nrm(ks[9], (DEPTH, SG_HEADS, SG_CHUNK), 0.1),
        "w_out": nrm(ks[10], (DEPTH, MIX_WIDTH, D_MODEL), MIX_WIDTH ** -0.5),
        "norm2_g": 1.0 + nrm(ks[11], (DEPTH, D_MODEL), 0.02),
        "w_ffn_in": nrm(ks[12], (DEPTH, D_MODEL, 2 * D_FF), D_MODEL ** -0.5),
        "ffn_conv_w": nrm(ks[13], (DEPTH, CONV_W, 2 * D_FF), CONV_W ** -0.5),
        "ffn_conv_b": nrm(ks[14], (DEPTH, 2 * D_FF), 0.02),
        "w_ffn_out": nrm(ks[15], (DEPTH, D_FF, D_MODEL), D_FF ** -0.5),
        "final_norm_g": 1.0 + nrm(ks[16], (D_MODEL,), 0.02),
    }


def reference(x_prompt, x_sample, norm1_g, w_in, hg_lb_logits, hg_norm_g, sg_norm_g, sg_norm_b,
              sg_w_spatial, sg_b_spatial, w_out, norm2_g, w_ffn_in, ffn_conv_w, ffn_conv_b,
              w_ffn_out, final_norm_g):
    y_prompt = _trunk(x_prompt, norm1_g, w_in, hg_lb_logits, hg_norm_g, sg_norm_g, sg_norm_b,
                      sg_w_spatial, sg_b_spatial, w_out, norm2_g, w_ffn_in, ffn_conv_w, ffn_conv_b,
                      w_ffn_out, final_norm_g)
    y_sample = _trunk(x_sample, norm1_g, w_in, hg_lb_logits, hg_norm_g, sg_norm_g, sg_norm_b,
                      sg_w_spatial, sg_b_spatial, w_out, norm2_g, w_ffn_in, ffn_conv_w, ffn_conv_b,
                      w_ffn_out, final_norm_g)
    return (y_prompt, y_sample)
```

```python
import functools

import jax
import jax.numpy as jnp
import numpy as np
from jax import lax
from jax.experimental import pallas as pl
from jax.experimental.pallas import tpu as pltpu

D_MODEL = 1024
HG_HEADS = 4
HG_DIM = 128
HG_WIDTH = HG_HEADS * HG_DIM
HG_CHUNK = 64
SG_HEADS = 4
SG_DIM = 128
SG_WIDTH = SG_HEADS * SG_DIM
SG_CHUNK = 128
D_FF = 2816
EPS = 1e-6

LEVELS = (32, 16, 8, 4, 2, 1)
N_SUMS = 2 + len(LEVELS)

FFN_COLS = 256
FFN_STEPS = D_FF // FFN_COLS
HALO = 16

MIX_TILE = 512
FFN_TILE = 512
VMEM_LIMIT = 56 * 1024 * 1024

F32 = jnp.float32
BF16 = jnp.bfloat16

NT_DIMS = (((1,), (1,)), ((), ()))
TN_DIMS = (((0,), (0,)), ((), ()))


def _hgrn_tables(reverse):
    c = HG_CHUNK
    r = np.arange(c)
    rr, tt = r[None, :], r[:, None]
    mats = [rr <= tt, rr > tt]
    sides, masks = [], []
    for m in LEVELS:
        blk = r // m
        odd = (blk % 2) == 1
        ref = np.where(odd, blk * m - 1, (blk + 1) * m - 1)[:, None]
        mats.append(np.where(odd[:, None], (rr > ref) & (rr <= tt), (rr > tt) & (rr <= ref)))
        sides.append(np.broadcast_to(odd[:, None], (c, HG_DIM)))
        masks.append(odd[:, None] & (blk[None, :] == blk[:, None] - 1))
    if reverse:
        mats = [a[::-1, ::-1] for a in mats]
        sides = [a[::-1] for a in sides]
        masks = [a[::-1, ::-1] for a in masks]
    sums = np.concatenate(mats, axis=0).astype(np.float32)
    sums3 = np.concatenate([sums, sums, sums], axis=1)
    return (jnp.asarray(sums3, BF16),
            jnp.asarray(np.concatenate(sides, axis=0), F32),
            jnp.asarray(np.concatenate(masks, axis=0), F32))


def _rms(x, g):
    return x * lax.rsqrt(jnp.mean(x * x, axis=-1, keepdims=True) + EPS) * g


def _sigmoid(x):
    return 1.0 / (1.0 + jnp.exp(-x))


def _gelu_tanh(x):
    return 0.5 * x * (1.0 + jnp.tanh(0.7978845608028654 * (x + 0.044715 * (x * x * x))))


def _lower_bound(logits, layer):
    mx = jnp.max(logits, axis=0, keepdims=True)
    e = jnp.exp(logits - mx)
    p = e / jnp.sum(e, axis=0, keepdims=True)
    lb = jnp.zeros((1, HG_WIDTH), F32)
    for j in range(1, layer + 1):
        lb = lb + p[j:j + 1, :]
    return lb


def _hgrn_chunk(q, z, v, lb, st, sums_ref, side_ref, mask_ref, last_row):
    c = HG_CHUNK
    sig = _sigmoid(z)
    f = lb + (1.0 - lb) * sig
    lf = jnp.log(f)
    k = (1.0 - lb) * (1.0 - sig)
    hi = lf.astype(BF16)
    r1 = lf - hi.astype(F32)
    mid = r1.astype(BF16)
    lo = (r1 - mid.astype(F32)).astype(BF16)
    parts = jnp.concatenate([hi, mid, lo], axis=0)
    decay = jnp.exp(jnp.dot(sums_ref[...], parts, preferred_element_type=F32))
    qs = q * decay[0:c]
    ks = k * decay[c:2 * c]
    d = decay[last_row:last_row + 1]
    scores = jnp.zeros((c, c), F32)
    for l in range(len(LEVELS)):
        g = decay[(2 + l) * c:(3 + l) * c]
        x = jnp.where(side_ref[l * c:(l + 1) * c, :] > 0.5, q, k) * g
        a = lax.dot_general(x, x, NT_DIMS, preferred_element_type=F32)
        scores = scores + a * mask_ref[l * c:(l + 1) * c, :]
    diag = jnp.sum(q * k, axis=-1, keepdims=True)
    o = (lax.dot_general(qs, st, NT_DIMS, preferred_element_type=F32)
         + jnp.dot(scores, v, preferred_element_type=F32) + diag * v)
    st_new = st * d + lax.dot_general(v, ks, TN_DIMS, preferred_element_type=F32)
    return o, st_new


def _hgrn_tile(proj_ref, o_ref, st_ref, lb, sums_ref, side_ref, mask_ref, *, tile, reverse,
               q_off, f_off, i_off):
    n_chunks = tile // HG_CHUNK
    last_row = 0 if reverse else HG_CHUNK - 1

    def body(ci, carry):
        cc = (n_chunks - 1 - ci) if reverse else ci
        rows = pl.ds(pl.multiple_of(cc * HG_CHUNK, HG_CHUNK), HG_CHUNK)
        for h in range(HG_HEADS):
            cols = slice(h * HG_DIM, (h + 1) * HG_DIM)
            q = proj_ref[rows, q_off + h * HG_DIM:q_off + (h + 1) * HG_DIM]
            z = proj_ref[rows, f_off + h * HG_DIM:f_off + (h + 1) * HG_DIM]
            v = proj_ref[rows, i_off + h * HG_DIM:i_off + (h + 1) * HG_DIM]
            o, st_new = _hgrn_chunk(q, z, v, lb[:, cols], st_ref[h], sums_ref, side_ref,
                                    mask_ref, last_row)
            st_ref[h] = st_new
            o_ref[rows, cols] = o
        return carry

    lax.fori_loop(0, n_chunks, body, 0)


def _bwd_kernel(x_ref, g1_ref, w_ref, logit_ref, sums_ref, side_ref, mask_ref, o_ref,
                proj_ref, st_ref, *, layer, tile):
    @pl.when(pl.program_id(1) == 0)
    def _():
        st_ref[...] = jnp.zeros_like(st_ref)

    h = _rms(x_ref[0], g1_ref[...]).astype(BF16)
    proj_ref[...] = jnp.dot(h, w_ref[...], preferred_element_type=F32)
    lb = _lower_bound(logit_ref[...], layer)
    _hgrn_tile(proj_ref, o_ref.at[0], st_ref, lb, sums_ref, side_ref, mask_ref, tile=tile,
               reverse=True, q_off=0, f_off=HG_WIDTH, i_off=2 * HG_WIDTH)


def _mix_kernel(x_ref, obw_ref, g1_ref, w_ref, logit_ref, sums_ref, side_ref, mask_ref,
                hgn_ref, sgn_g_ref, sgn_b_ref, ws_ref, bs_ref, wout_ref, o_ref,
                proj_ref, ofw_ref, mix_ref, st_ref, *, layer, tile):
    @pl.when(pl.program_id(1) == 0)
    def _():
        st_ref[...] = jnp.zeros_like(st_ref)

    x = x_ref[0]
    h = _rms(x, g1_ref[...]).astype(BF16)
    proj_ref[...] = jnp.dot(h, w_ref[...], preferred_element_type=F32)
    lb = _lower_bound(logit_ref[...], layer)
    _hgrn_tile(proj_ref, ofw_ref, st_ref, lb, sums_ref, side_ref, mask_ref, tile=tile,
               reverse=False, q_off=0, f_off=HG_WIDTH, i_off=2 * HG_WIDTH)

    g_off, u_off, v_off = 3 * HG_WIDTH, 4 * HG_WIDTH, 4 * HG_WIDTH + SG_WIDTH
    for hd in range(HG_HEADS):
        cols = slice(hd * HG_DIM, (hd + 1) * HG_DIM)
        o = ofw_ref[:, cols] + obw_ref[0, :, cols]
        o = _rms(o, hgn_ref[:, cols])
        gate = proj_ref[:, g_off + hd * HG_DIM:g_off + (hd + 1) * HG_DIM]
        mix_ref[:, cols] = (o * (gate * _sigmoid(gate))).astype(BF16)

    for hd in range(SG_HEADS):
        cols = slice(hd * SG_DIM, (hd + 1) * SG_DIM)
        bias = jnp.broadcast_to(bs_ref[hd], (SG_CHUNK, SG_DIM))
        for pc in range(tile // SG_CHUNK):
            rows = slice(pc * SG_CHUNK, (pc + 1) * SG_CHUNK)
            vv = _gelu_tanh(proj_ref[rows, v_off + hd * SG_DIM:v_off + (hd + 1) * SG_DIM])
            mu = jnp.mean(vv, axis=-1, keepdims=True)
            vc = vv - mu
            vn = vc * lax.rsqrt(jnp.mean(vc * vc, axis=-1, keepdims=True) + EPS)
            vn = vn * sgn_g_ref[:, cols] + sgn_b_ref[:, cols]
            sp = jnp.dot(ws_ref[hd], vn, preferred_element_type=F32) + bias
            u = _gelu_tanh(proj_ref[rows, u_off + hd * SG_DIM:u_off + (hd + 1) * SG_DIM])
            mix_ref[rows, HG_WIDTH + hd * SG_DIM:HG_WIDTH + (hd + 1) * SG_DIM] = (
                u * sp).astype(BF16)

    o_ref[0] = x + jnp.dot(mix_ref[...], wout_ref[...], preferred_element_type=F32)


def _ffn_kernel(x_ref, xp_ref, xn_ref, g2_ref, win_ref, cw_ref, cb_ref, wout_ref, gf_ref,
                o_ref, h_ref, a_ref, act_ref, *, tile, final_norm):
    j = pl.program_id(1)
    n = pl.num_programs(1)
    x = x_ref[0]
    g2 = g2_ref[...]
    h_ref[0:tile, :] = _rms(x, g2).astype(BF16)
    hp = _rms(xp_ref[0, HALO - 1:HALO, :], g2) * jnp.where(j > 0, 1.0, 0.0)
    hn = _rms(xn_ref[0, 0:1, :], g2) * jnp.where(j < n - 1, 1.0, 0.0)
    rid = lax.broadcasted_iota(jnp.int32, (HALO, D_MODEL), 0)
    halo = jnp.where(rid == 0, hp, jnp.where(rid == 1, hn, 0.0))
    h_ref[tile:tile + HALO, :] = halo.astype(BF16)

    def step(s, carry):
        a = jnp.dot(h_ref[...], win_ref[s], preferred_element_type=F32)
        a_ref[8:8 + tile, :] = a[0:tile]
        a_ref[7:8, :] = a[tile:tile + 1]
        a_ref[8 + tile:9 + tile, :] = a[tile + 1:tile + 2]
        cw = cw_ref[s]
        y = (a_ref[7:7 + tile, :] * cw[0:1] + a_ref[8:8 + tile, :] * cw[1:2]
             + a_ref[9:9 + tile, :] * cw[2:3] + cb_ref[s])
        gate = y[:, 0:FFN_COLS]
        up = y[:, FFN_COLS:2 * FFN_COLS]
        act_ref[s] = (gate * _sigmoid(gate) * up).astype(BF16)
        return carry

    lax.fori_loop(0, FFN_STEPS, step, 0)

    acc = x
    for s in range(FFN_STEPS):
        acc = acc + jnp.dot(act_ref[s], wout_ref[s], preferred_element_type=F32)
    if final_norm:
        acc = _rms(acc, gf_ref[...])
    o_ref[0] = acc


def _const_spec(shape):
    nd = len(shape)
    return pl.BlockSpec(shape, lambda b, j: (0,) * nd, pipeline_mode=pl.Buffered(1))


def _params():
    return pltpu.CompilerParams(dimension_semantics=("arbitrary", "arbitrary"),
                                vmem_limit_bytes=VMEM_LIMIT)


def _bwd_call(x, g1, w_bwd, logits, tables, *, layer):
    bsz, seq, _ = x.shape
    tile = MIX_TILE
    nt = seq // tile
    sums, side, mask = tables
    return pl.pallas_call(
        functools.partial(_bwd_kernel, layer=layer, tile=tile),
        out_shape=jax.ShapeDtypeStruct((bsz, seq, HG_WIDTH), F32),
        grid=(bsz, nt),
        in_specs=[
            pl.BlockSpec((1, tile, D_MODEL), lambda b, j: (b, nt - 1 - j, 0)),
            _const_spec(g1.shape), _const_spec(w_bwd.shape), _const_spec(logits.shape),
            _const_spec(sums.shape), _const_spec(side.shape), _const_spec(mask.shape),
        ],
        out_specs=pl.BlockSpec((1, tile, HG_WIDTH), lambda b, j: (b, nt - 1 - j, 0)),
        scratch_shapes=[
            pltpu.VMEM((tile, 3 * HG_WIDTH), F32),
            pltpu.VMEM((HG_HEADS, HG_DIM, HG_DIM), F32),
        ],
        compiler_params=_params(),
        name=f"hgrn_bwd_l{layer}",
    )(x, g1, w_bwd, logits, sums, side, mask)


def _mix_call(x, obw, g1, w_fwd, logits, tables, hgn, sgn_g, sgn_b, ws, bs, wout, *, layer):
    bsz, seq, _ = x.shape
    tile = MIX_TILE
    nt = seq // tile
    sums, side, mask = tables
    consts = (g1, w_fwd, logits, sums, side, mask, hgn, sgn_g, sgn_b, ws, bs, wout)
    return pl.pallas_call(
        functools.partial(_mix_kernel, layer=layer, tile=tile),
        out_shape=jax.ShapeDtypeStruct((bsz, seq, D_MODEL), F32),
        grid=(bsz, nt),
        in_specs=[
            pl.BlockSpec((1, tile, D_MODEL), lambda b, j: (b, j, 0)),
            pl.BlockSpec((1, tile, HG_WIDTH), lambda b, j: (b, j, 0)),
        ] + [_const_spec(c.shape) for c in consts],
        out_specs=pl.BlockSpec((1, tile, D_MODEL), lambda b, j: (b, j, 0)),
        scratch_shapes=[
            pltpu.VMEM((tile, 4 * HG_WIDTH + 2 * SG_WIDTH), F32),
            pltpu.VMEM((tile, HG_WIDTH), F32),
            pltpu.VMEM((tile, D_MODEL), BF16),
            pltpu.VMEM((HG_HEADS, HG_DIM, HG_DIM), F32),
        ],
        compiler_params=_params(),
        name=f"mixer_l{layer}",
    )(x, obw, *consts)


def _ffn_call(x, g2, win, cw, cb, wout, gf, *, layer, final_norm):
    bsz, seq, _ = x.shape
    tile = FFN_TILE
    nt = seq // tile
    per = tile // HALO
    last = seq // HALO - 1
    consts = (g2, win, cw, cb, wout, gf)
    return pl.pallas_call(
        functools.partial(_ffn_kernel, tile=tile, final_norm=final_norm),
        out_shape=jax.ShapeDtypeStruct((bsz, seq, D_MODEL), F32),
        grid=(bsz, nt),
        in_specs=[
            pl.BlockSpec((1, tile, D_MODEL), lambda b, j: (b, j, 0)),
            pl.BlockSpec((1, HALO, D_MODEL), lambda b, j: (b, jnp.maximum(j * per - 1, 0), 0)),
            pl.BlockSpec((1, HALO, D_MODEL),
                         lambda b, j: (b, jnp.minimum((j + 1) * per, last), 0)),
        ] + [_const_spec(c.shape) for c in consts],
        out_specs=pl.BlockSpec((1, tile, D_MODEL), lambda b, j: (b, j, 0)),
        scratch_shapes=[
            pltpu.VMEM((tile + HALO, D_MODEL), BF16),
            pltpu.VMEM((tile + 16, 2 * FFN_COLS), F32),
            pltpu.VMEM((FFN_STEPS, tile, FFN_COLS), BF16),
        ],
        compiler_params=_params(),
        name=f"ffn_l{layer}",
    )(x, x, x, *consts)


def _trunk(x, layers, final_g, tables_fwd, tables_bwd):
    depth = len(layers)
    for l, p in enumerate(layers):
        obw = _bwd_call(x, p["g1"], p["w_bwd"], p["logits_bwd"], tables_bwd, layer=l)
        x = _mix_call(x, obw, p["g1"], p["w_fwd"], p["logits_fwd"], tables_fwd, p["hgn"],
                      p["sgn_g"], p["sgn_b"], p["ws"], p["bs"], p["wout"], layer=l)
        x = _ffn_call(x, p["g2"], p["win"], p["cw"], p["cb"], p["wffn_out"], final_g,
                      layer=l, final_norm=(l == depth - 1))
    return x


def kernel(x_prompt, x_sample, norm1_g, w_in, hg_lb_logits, hg_norm_g, sg_norm_g, sg_norm_b,
           sg_w_spatial, sg_b_spatial, w_out, norm2_g, w_ffn_in, ffn_conv_w, ffn_conv_b,
           w_ffn_out, final_norm_g):
    depth = w_in.shape[0]
    hw = HG_WIDTH
    layers = []
    for l in range(depth):
        wl = w_in[l]
        q, f_fw, f_bw, i_in, rest = (wl[:, 0:hw], wl[:, hw:2 * hw], wl[:, 2 * hw:3 * hw],
                                     wl[:, 3 * hw:4 * hw], wl[:, 4 * hw:])
        gate_w = w_ffn_in[l][:, :D_FF].reshape(D_MODEL, FFN_STEPS, FFN_COLS)
        up_w = w_ffn_in[l][:, D_FF:].reshape(D_MODEL, FFN_STEPS, FFN_COLS)
        cw = ffn_conv_w[l]
        cb = ffn_conv_b[l]
        layers.append(dict(
            g1=norm1_g[l].reshape(1, D_MODEL),
            w_fwd=jnp.concatenate([q, f_fw, i_in, rest], axis=1).astype(BF16),
            w_bwd=jnp.concatenate([q, f_bw, i_in], axis=1).astype(BF16),
            logits_fwd=hg_lb_logits[0].astype(F32),
            logits_bwd=hg_lb_logits[1].astype(F32),
            hgn=hg_norm_g[l].reshape(1, hw),
            sgn_g=sg_norm_g[l].reshape(1, SG_WIDTH),
            sgn_b=sg_norm_b[l].reshape(1, SG_WIDTH),
            ws=sg_w_spatial[l],
            bs=sg_b_spatial[l].reshape(SG_HEADS, SG_CHUNK, 1),
            wout=w_out[l].astype(BF16),
            g2=norm2_g[l].reshape(1, D_MODEL),
            win=jnp.concatenate([gate_w, up_w], axis=2).transpose(1, 0, 2).astype(BF16),
            cw=jnp.concatenate([cw[:, :D_FF].reshape(3, FFN_STEPS, FFN_COLS),
                                cw[:, D_FF:].reshape(3, FFN_STEPS, FFN_COLS)],
                               axis=2).transpose(1, 0, 2),
            cb=jnp.concatenate([cb[:D_FF].reshape(FFN_STEPS, 1, FFN_COLS),
                                cb[D_FF:].reshape(FFN_STEPS, 1, FFN_COLS)], axis=2),
            wffn_out=w_ffn_out[l].reshape(FFN_STEPS, FFN_COLS, D_MODEL).astype(BF16),
        ))
    final_g = final_norm_g.reshape(1, D_MODEL)
    tables_fwd = _hgrn_tables(False)
    tables_bwd = _hgrn_tables(True)
    y_prompt = _trunk(x_prompt, layers, final_g, tables_fwd, tables_bwd)
    y_sample = _trunk(x_sample, layers, final_g, tables_fwd, tables_bwd)
    return (y_prompt, y_sample)
```

```python
import functools

import jax
import jax.numpy as jnp
import numpy as np
from jax import lax
from jax.experimental import pallas as pl
from jax.experimental.pallas import tpu as pltpu

D_MODEL = 1024
HG_HEADS = 4
HG_DIM = 128
HG_WIDTH = HG_HEADS * HG_DIM
HG_CHUNK = 64
SG_HEADS = 4
SG_DIM = 128
SG_WIDTH = SG_HEADS * SG_DIM
SG_CHUNK = 128
D_FF = 2816
EPS = 1e-6
LOG2E = 1.4426950408889634

LANES = 128
SUBLANES = 8

LEVELS = (32, 16, 8, 4, 2)
N_LEVELS = len(LEVELS) + 1

FFN_COLS = 256
FFN_STEPS = D_FF // FFN_COLS
N_SLABS = 2 * FFN_COLS // LANES
HALO = 16

MIX_TILE = 512
FFN_TILE = 512
VMEM_LIMIT = 56 * 1024 * 1024

F32 = jnp.float32
BF16 = jnp.bfloat16

NT_DIMS = (((1,), (1,)), ((), ()))
TN_DIMS = (((0,), (0,)), ((), ()))


def _is_query_side(block_index, reverse):
    return ((block_index % 2) == 1) != reverse


def _hgrn_tables(reverse):
    c = HG_CHUNK
    r = np.arange(c)
    rr, tt = r[None, :], r[:, None]
    mats = [rr <= tt]
    masks = []
    for m in LEVELS + (1,):
        blk = r // m
        odd = (blk % 2) == 1
        ref = np.where(odd, blk * m - 1, (blk + 1) * m - 1)[:, None]
        if m > 1:
            mats.append(np.where(odd[:, None], (rr > ref) & (rr <= tt), (rr > tt) & (rr <= ref)))
        masks.append(odd[:, None] & (blk[None, :] == blk[:, None] - 1))
    if reverse:
        mats = [a[::-1, ::-1] for a in mats]
        masks = [a[::-1, ::-1] for a in masks]
    sums = np.concatenate(mats, axis=0).astype(np.float32)
    sums2 = np.concatenate([sums, sums], axis=1)
    return (jnp.asarray(sums2, BF16), jnp.asarray(np.concatenate(masks, axis=0), F32))


def _rms(x, g):
    return x * lax.rsqrt(jnp.mean(x * x, axis=-1, keepdims=True) + EPS) * g


def _sigmoid(x):
    return 1.0 / (1.0 + jnp.exp(-x))


def _gelu_tanh(x):
    return 0.5 * x * (1.0 + jnp.tanh(0.7978845608028654 * (x + 0.044715 * (x * x * x))))


def _lower_bound(logits, layer):
    mx = jnp.max(logits, axis=0, keepdims=True)
    e = jnp.exp(logits - mx)
    p = e / jnp.sum(e, axis=0, keepdims=True)
    lb = jnp.zeros((1, HG_WIDTH), F32)
    for j in range(1, layer + 1):
        lb = lb + p[j:j + 1, :]
    return lb


def _hgrn_tile(proj_ref, o_ref, st_ref, xl_ref, qs_ref, ks_ref, d_ref, sc_ref, lb,
               sums_ref, mask_ref, *, tile, reverse, q_off, f_off, i_off):
    c = HG_CHUNK
    n_chunks = tile // c
    last_row = 0 if reverse else c - 1

    def rows_of(ci):
        return pl.ds(pl.multiple_of(ci * c, c), c)

    def scale_pass(ci, carry):
        rows = rows_of(ci)
        q = proj_ref[rows, q_off:q_off + HG_WIDTH]
        z = proj_ref[rows, f_off:f_off + HG_WIDTH]
        v = proj_ref[rows, i_off:i_off + HG_WIDTH]
        sig = _sigmoid(z)
        f = lb + (1.0 - lb) * sig
        k = (1.0 - lb) * (1.0 - sig)
        lf2 = jnp.log(f) * LOG2E
        hi = lf2.astype(BF16)
        lo = (lf2 - hi.astype(F32)).astype(BF16)
        parts = jnp.concatenate([hi, lo], axis=0)
        e = jnp.dot(sums_ref[...], parts, preferred_element_type=F32)
        cum = e[0:c]
        tot = cum[last_row:last_row + 1]
        qs_ref[rows, :] = q * jnp.exp2(cum)
        ks_ref[rows, :] = k * jnp.exp2(tot - cum)
        d_ref[ci] = jnp.exp2(tot)
        row = lax.broadcasted_iota(jnp.int32, (c, HG_WIDTH), 0)
        for l, m in enumerate(LEVELS + (1,)):
            if m >= SUBLANES:
                qk = jnp.concatenate(
                    [(q if _is_query_side(r0 // m, reverse) else k)[r0:r0 + m]
                     for r0 in range(0, c, m)], axis=0)
            else:
                qk = jnp.where(_is_query_side(row // m, reverse), q * f if m == 1 else q, k)
            if m > 1:
                qk = qk * jnp.exp2(e[(1 + l) * c:(2 + l) * c])
            xl_ref[l, rows, :] = qk.astype(BF16)
        for h in range(HG_HEADS):
            cols = slice(h * HG_DIM, (h + 1) * HG_DIM)
            diag = jnp.sum(q[:, cols] * k[:, cols], axis=-1, keepdims=True)
            o_ref[rows, cols] = diag * v[:, cols]
        return carry

    def score_pass(ci, carry):
        rows = rows_of(ci)
        for h in range(HG_HEADS):
            cols = slice(h * HG_DIM, (h + 1) * HG_DIM)
            scores = jnp.zeros((c, c), F32)
            for l in range(N_LEVELS):
                x = xl_ref[l, rows, cols]
                a = lax.dot_general(x, x, NT_DIMS, preferred_element_type=F32)
                scores = scores + a * mask_ref[l * c:(l + 1) * c, :]
            sc_ref[ci, h] = scores
        return carry

    def state_pass(i, carry):
        ci = (n_chunks - 1 - i) if reverse else i
        rows = rows_of(ci)
        d = d_ref[ci]
        for h in range(HG_HEADS):
            cols = slice(h * HG_DIM, (h + 1) * HG_DIM)
            v = proj_ref[rows, i_off + h * HG_DIM:i_off + (h + 1) * HG_DIM]
            st = st_ref[h]
            o = (lax.dot_general(qs_ref[rows, cols], st, NT_DIMS, preferred_element_type=F32)
                 + jnp.dot(sc_ref[ci, h], v, preferred_element_type=F32))
            o_ref[rows, cols] += o
            st_ref[h] = st * d[:, cols] + lax.dot_general(v, ks_ref[rows, cols], TN_DIMS,
                                                          preferred_element_type=F32)
        return carry

    lax.fori_loop(0, n_chunks, scale_pass, 0, unroll=2)
    lax.fori_loop(0, n_chunks, score_pass, 0, unroll=4)
    lax.fori_loop(0, n_chunks, state_pass, 0, unroll=4)


def _hgrn_scratch(tile):
    n_chunks = tile // HG_CHUNK
    return [
        pltpu.VMEM((HG_HEADS, HG_DIM, HG_DIM), F32),
        pltpu.VMEM((N_LEVELS, tile, HG_WIDTH), BF16),
        pltpu.VMEM((tile, HG_WIDTH), F32),
        pltpu.VMEM((tile, HG_WIDTH), F32),
        pltpu.VMEM((n_chunks, 1, HG_WIDTH), F32),
        pltpu.VMEM((n_chunks, HG_HEADS, HG_CHUNK, HG_CHUNK), F32),
    ]


def _bwd_kernel(x_ref, g1_ref, w_ref, logit_ref, sums_ref, mask_ref, o_ref,
                proj_ref, st_ref, xl_ref, qs_ref, ks_ref, d_ref, sc_ref, *, layer, tile):
    @pl.when(pl.program_id(1) == 0)
    def _():
        st_ref[...] = jnp.zeros_like(st_ref)

    h = _rms(x_ref[0], g1_ref[...]).astype(BF16)
    proj_ref[...] = jnp.dot(h, w_ref[...], preferred_element_type=F32)
    lb = _lower_bound(logit_ref[...], layer)
    _hgrn_tile(proj_ref, o_ref.at[0], st_ref, xl_ref, qs_ref, ks_ref, d_ref, sc_ref, lb,
               sums_ref, mask_ref, tile=tile, reverse=True,
               q_off=0, f_off=HG_WIDTH, i_off=2 * HG_WIDTH)


def _mix_kernel(x_ref, obw_ref, g1_ref, w_ref, logit_ref, sums_ref, mask_ref,
                hgn_ref, sgn_g_ref, sgn_b_ref, ws_ref, bs_ref, wout_ref, o_ref,
                proj_ref, ofw_ref, mix_ref, st_ref, xl_ref, qs_ref, ks_ref, d_ref, sc_ref,
                *, layer, tile):
    @pl.when(pl.program_id(1) == 0)
    def _():
        st_ref[...] = jnp.zeros_like(st_ref)

    x = x_ref[0]
    h = _rms(x, g1_ref[...]).astype(BF16)
    proj_ref[...] = jnp.dot(h, w_ref[...], preferred_element_type=F32)
    lb = _lower_bound(logit_ref[...], layer)
    _hgrn_tile(proj_ref, ofw_ref, st_ref, xl_ref, qs_ref, ks_ref, d_ref, sc_ref, lb,
               sums_ref, mask_ref, tile=tile, reverse=False,
               q_off=0, f_off=HG_WIDTH, i_off=2 * HG_WIDTH)

    g_off, u_off, v_off = 3 * HG_WIDTH, 4 * HG_WIDTH, 4 * HG_WIDTH + SG_WIDTH
    for hd in range(HG_HEADS):
        cols = slice(hd * HG_DIM, (hd + 1) * HG_DIM)
        o = ofw_ref[:, cols] + obw_ref[0, :, cols]
        o = _rms(o, hgn_ref[:, cols])
        gate = proj_ref[:, g_off + hd * HG_DIM:g_off + (hd + 1) * HG_DIM]
        mix_ref[:, cols] = (o * (gate * _sigmoid(gate))).astype(BF16)

    for hd in range(SG_HEADS):
        cols = slice(hd * SG_DIM, (hd + 1) * SG_DIM)
        bias = jnp.broadcast_to(bs_ref[hd], (SG_CHUNK, SG_DIM))
        for pc in range(tile // SG_CHUNK):
            rows = slice(pc * SG_CHUNK, (pc + 1) * SG_CHUNK)
            vv = _gelu_tanh(proj_ref[rows, v_off + hd * SG_DIM:v_off + (hd + 1) * SG_DIM])
            mu = jnp.mean(vv, axis=-1, keepdims=True)
            vc = vv - mu
            vn = vc * lax.rsqrt(jnp.mean(vc * vc, axis=-1, keepdims=True) + EPS)
            vn = vn * sgn_g_ref[:, cols] + sgn_b_ref[:, cols]
            sp = jnp.dot(ws_ref[hd], vn, preferred_element_type=F32) + bias
            u = _gelu_tanh(proj_ref[rows, u_off + hd * SG_DIM:u_off + (hd + 1) * SG_DIM])
            mix_ref[rows, HG_WIDTH + hd * SG_DIM:HG_WIDTH + (hd + 1) * SG_DIM] = (
                u * sp).astype(BF16)

    o_ref[0] = x + jnp.dot(mix_ref[...], wout_ref[...], preferred_element_type=F32)


def _ffn_kernel(x_ref, xp_ref, xn_ref, g2_ref, win_ref, cw_ref, cb_ref, wout_ref, gf_ref,
                o_ref, h_ref, a_ref, act_ref, *, tile, final_norm):
    j = pl.program_id(1)
    n = pl.num_programs(1)
    x = x_ref[0]
    g2 = g2_ref[...]
    h_ref[HALO:HALO + tile, :] = _rms(x, g2).astype(BF16)
    h_ref[0:HALO, :] = (_rms(xp_ref[0], g2) * jnp.where(j > 0, 1.0, 0.0)).astype(BF16)
    h_ref[HALO + tile:2 * HALO + tile, :] = (
        _rms(xn_ref[0], g2) * jnp.where(j < n - 1, 1.0, 0.0)).astype(BF16)

    def up_proj(s):
        a = jnp.dot(h_ref[...], win_ref[s], preferred_element_type=F32)
        for cs in range(N_SLABS):
            a_ref[s % 2, cs] = a[:, cs * LANES:(cs + 1) * LANES]

    def conv_gate(s):
        cw = cw_ref[s]
        cb = cb_ref[s]
        ys = []
        for cs in range(N_SLABS):
            cols = slice(cs * LANES, (cs + 1) * LANES)
            ys.append(a_ref[s % 2, cs, pl.ds(HALO - 1, tile), :] * cw[0:1, cols]
                      + a_ref[s % 2, cs, pl.ds(HALO, tile), :] * cw[1:2, cols]
                      + a_ref[s % 2, cs, pl.ds(HALO + 1, tile), :] * cw[2:3, cols]
                      + cb[:, cols])
        half = N_SLABS // 2
        gate = jnp.concatenate(ys[:half], axis=1)
        up = jnp.concatenate(ys[half:], axis=1)
        act_ref[s] = (gate * _sigmoid(gate) * up).astype(BF16)

    up_proj(0)
    for s in range(FFN_STEPS):
        if s + 1 < FFN_STEPS:
            up_proj(s + 1)
        conv_gate(s)

    acc = x
    for s in range(FFN_STEPS):
        acc = acc + jnp.dot(act_ref[s], wout_ref[s], preferred_element_type=F32)
    if final_norm:
        acc = _rms(acc, gf_ref[...])
    o_ref[0] = acc


def _const_spec(shape):
    nd = len(shape)
    return pl.BlockSpec(shape, lambda b, j: (0,) * nd, pipeline_mode=pl.Buffered(1))


def _params():
    return pltpu.CompilerParams(dimension_semantics=("arbitrary", "arbitrary"),
                                vmem_limit_bytes=VMEM_LIMIT)


def _bwd_call(x, g1, w_bwd, logits, tables, *, layer):
    bsz, seq, _ = x.shape
    tile = MIX_TILE
    nt = seq // tile
    consts = (g1, w_bwd, logits) + tuple(tables)
    return pl.pallas_call(
        functools.partial(_bwd_kernel, layer=layer, tile=tile),
        out_shape=jax.ShapeDtypeStruct((bsz, seq, HG_WIDTH), F32),
        grid=(bsz, nt),
        in_specs=[pl.BlockSpec((1, tile, D_MODEL), lambda b, j: (b, nt - 1 - j, 0))]
        + [_const_spec(c.shape) for c in consts],
        out_specs=pl.BlockSpec((1, tile, HG_WIDTH), lambda b, j: (b, nt - 1 - j, 0)),
        scratch_shapes=[pltpu.VMEM((tile, 3 * HG_WIDTH), F32)] + _hgrn_scratch(tile),
        compiler_params=_params(),
        name=f"hgrn_bwd_l{layer}",
    )(x, *consts)


def _mix_call(x, obw, g1, w_fwd, logits, tables, hgn, sgn_g, sgn_b, ws, bs, wout, *, layer):
    bsz, seq, _ = x.shape
    tile = MIX_TILE
    nt = seq // tile
    consts = (g1, w_fwd, logits) + tuple(tables) + (hgn, sgn_g, sgn_b, ws, bs, wout)
    return pl.pallas_call(
        functools.partial(_mix_kernel, layer=layer, tile=tile),
        out_shape=jax.ShapeDtypeStruct((bsz, seq, D_MODEL), F32),
        grid=(bsz, nt),
        in_specs=[
            pl.BlockSpec((1, tile, D_MODEL), lambda b, j: (b, j, 0)),
            pl.BlockSpec((1, tile, HG_WIDTH), lambda b, j: (b, j, 0)),
        ] + [_const_spec(c.shape) for c in consts],
        out_specs=pl.BlockSpec((1, tile, D_MODEL), lambda b, j: (b, j, 0)),
        scratch_shapes=[
            pltpu.VMEM((tile, 4 * HG_WIDTH + 2 * SG_WIDTH), F32),
            pltpu.VMEM((tile, HG_WIDTH), F32),
            pltpu.VMEM((tile, D_MODEL), BF16),
        ] + _hgrn_scratch(tile),
        compiler_params=_params(),
        name=f"mixer_l{layer}",
    )(x, obw, *consts)


def _ffn_call(x, g2, win, cw, cb, wout, gf, *, layer, final_norm):
    bsz, seq, _ = x.shape
    tile = FFN_TILE
    nt = seq // tile
    per = tile // HALO
    last = seq // HALO - 1
    consts = (g2, win, cw, cb, wout, gf)
    return pl.pallas_call(
        functools.partial(_ffn_kernel, tile=tile, final_norm=final_norm),
        out_shape=jax.ShapeDtypeStruct((bsz, seq, D_MODEL), F32),
        grid=(bsz, nt),
        in_specs=[
            pl.BlockSpec((1, tile, D_MODEL), lambda b, j: (b, j, 0)),
            pl.BlockSpec((1, HALO, D_MODEL), lambda b, j: (b, jnp.maximum(j * per - 1, 0), 0)),
            pl.BlockSpec((1, HALO, D_MODEL),
                         lambda b, j: (b, jnp.minimum((j + 1) * per, last), 0)),
        ] + [_const_spec(c.shape) for c in consts],
        out_specs=pl.BlockSpec((1, tile, D_MODEL), lambda b, j: (b, j, 0)),
        scratch_shapes=[
            pltpu.VMEM((tile + 2 * HALO, D_MODEL), BF16),
            pltpu.VMEM((2, N_SLABS, tile + 2 * HALO, LANES), F32),
            pltpu.VMEM((FFN_STEPS, tile, FFN_COLS), BF16),
        ],
        compiler_params=_params(),
        name=f"ffn_l{layer}",
    )(x, x, x, *consts)


def _trunk(x, layers, final_g, tables_fwd, tables_bwd):
    depth = len(layers)
    for l, p in enumerate(layers):
        obw = _bwd_call(x, p["g1"], p["w_bwd"], p["logits_bwd"], tables_bwd, layer=l)
        x = _mix_call(x, obw, p["g1"], p["w_fwd"], p["logits_fwd"], tables_fwd, p["hgn"],
                      p["sgn_g"], p["sgn_b"], p["ws"], p["bs"], p["wout"], layer=l)
        x = _ffn_call(x, p["g2"], p["win"], p["cw"], p["cb"], p["wffn_out"], final_g,
                      layer=l, final_norm=(l == depth - 1))
    return x


def kernel(x_prompt, x_sample, norm1_g, w_in, hg_lb_logits, hg_norm_g, sg_norm_g, sg_norm_b,
           sg_w_spatial, sg_b_spatial, w_out, norm2_g, w_ffn_in, ffn_conv_w, ffn_conv_b,
           w_ffn_out, final_norm_g):
    depth = w_in.shape[0]
    hw = HG_WIDTH
    layers = []
    for l in range(depth):
        wl = w_in[l]
        q, f_fw, f_bw, i_in, rest = (wl[:, 0:hw], wl[:, hw:2 * hw], wl[:, 2 * hw:3 * hw],
                                     wl[:, 3 * hw:4 * hw], wl[:, 4 * hw:])
        gate_w = w_ffn_in[l][:, :D_FF].reshape(D_MODEL, FFN_STEPS, FFN_COLS)
        up_w = w_ffn_in[l][:, D_FF:].reshape(D_MODEL, FFN_STEPS, FFN_COLS)
        cw = ffn_conv_w[l]
        cb = ffn_conv_b[l]
        layers.append(dict(
            g1=norm1_g[l].reshape(1, D_MODEL),
            w_fwd=jnp.concatenate([q, f_fw, i_in, rest], axis=1).astype(BF16),
            w_bwd=jnp.concatenate([q, f_bw, i_in], axis=1).astype(BF16),
            logits_fwd=hg_lb_logits[0].astype(F32),
            logits_bwd=hg_lb_logits[1].astype(F32),
            hgn=hg_norm_g[l].reshape(1, hw),
            sgn_g=sg_norm_g[l].reshape(1, SG_WIDTH),
            sgn_b=sg_norm_b[l].reshape(1, SG_WIDTH),
            ws=sg_w_spatial[l],
            bs=sg_b_spatial[l].reshape(SG_HEADS, SG_CHUNK, 1),
            wout=w_out[l].astype(BF16),
            g2=norm2_g[l].reshape(1, D_MODEL),
            win=jnp.concatenate([gate_w, up_w], axis=2).transpose(1, 0, 2).astype(BF16),
            cw=jnp.concatenate([cw[:, :D_FF].reshape(3, FFN_STEPS, FFN_COLS),
                                cw[:, D_FF:].reshape(3, FFN_STEPS, FFN_COLS)],
                               axis=2).transpose(1, 0, 2),
            cb=jnp.concatenate([cb[:D_FF].reshape(FFN_STEPS, 1, FFN_COLS),
                                cb[D_FF:].reshape(FFN_STEPS, 1, FFN_COLS)], axis=2),
            wffn_out=w_ffn_out[l].reshape(FFN_STEPS, FFN_COLS, D_MODEL).astype(BF16),
        ))
    final_g = final_norm_g.reshape(1, D_MODEL)
    tables_fwd = _hgrn_tables(False)
    tables_bwd = _hgrn_tables(True)
    y_prompt = _trunk(x_prompt, layers, final_g, tables_fwd, tables_bwd)
    y_sample = _trunk(x_sample, layers, final_g, tables_fwd, tables_bwd)
    return (y_prompt, y_sample)
```

```python
import functools

import jax
import jax.numpy as jnp
import numpy as np
from jax import lax
from jax.experimental import pallas as pl
from jax.experimental.pallas import tpu as pltpu

D_MODEL = 1024
HG_HEADS = 4
HG_DIM = 128
HG_WIDTH = HG_HEADS * HG_DIM
HG_CHUNK = 64
SG_HEADS = 4
SG_DIM = 128
SG_WIDTH = SG_HEADS * SG_DIM
SG_CHUNK = 128
D_FF = 2816
EPS = 1e-6
LOG2E = 1.4426950408889634

LANES = 128
SUBLANES = 8

LEVELS = (32, 16, 8, 4, 2)
N_LEVELS = len(LEVELS) + 1

FFN_COLS = 256
FFN_STEPS = D_FF // FFN_COLS
N_SLABS = 2 * FFN_COLS // LANES
HALO = 16

MIX_TILE = 512
OUT_ROWS = 128
FFN_TILE = 512
VMEM_LIMIT = 56 * 1024 * 1024

F32 = jnp.float32
BF16 = jnp.bfloat16

NT_DIMS = (((1,), (1,)), ((), ()))
TN_DIMS = (((0,), (0,)), ((), ()))


def _is_query_side(block_index, reverse):
    return ((block_index % 2) == 1) != reverse


def _hgrn_tables(reverse):
    c = HG_CHUNK
    r = np.arange(c)
    rr, tt = r[None, :], r[:, None]
    mats = [rr <= tt]
    masks = []
    for m in LEVELS + (1,):
        blk = r // m
        odd = (blk % 2) == 1
        ref = np.where(odd, blk * m - 1, (blk + 1) * m - 1)[:, None]
        if m > 1:
            mats.append(np.where(odd[:, None], (rr > ref) & (rr <= tt), (rr > tt) & (rr <= ref)))
        masks.append(odd[:, None] & (blk[None, :] == blk[:, None] - 1))
    if reverse:
        mats = [a[::-1, ::-1] for a in mats]
        masks = [a[::-1, ::-1] for a in masks]
    sums = np.concatenate(mats, axis=0).astype(np.float32)
    sums2 = np.concatenate([sums, sums], axis=1)
    return (jnp.asarray(sums2, BF16), jnp.asarray(np.concatenate(masks, axis=0), F32))


def _rms(x, g):
    return x * lax.rsqrt(jnp.mean(x * x, axis=-1, keepdims=True) + EPS) * g


def _sigmoid(x):
    return 1.0 / (1.0 + jnp.exp(-x))


def _gelu_tanh(x):
    return 0.5 * x * (1.0 + jnp.tanh(0.7978845608028654 * (x + 0.044715 * (x * x * x))))


def _lower_bound(logits, layer):
    mx = jnp.max(logits, axis=0, keepdims=True)
    e = jnp.exp(logits - mx)
    p = e / jnp.sum(e, axis=0, keepdims=True)
    lb = jnp.zeros((1, HG_WIDTH), F32)
    for j in range(1, layer + 1):
        lb = lb + p[j:j + 1, :]
    return lb


def _hgrn_tile(proj_ref, o_ref, st_ref, xl_ref, qs_ref, ks_ref, d_ref, sc_ref, lb,
               sums_ref, mask_ref, *, tile, reverse, q_off, f_off, i_off):
    c = HG_CHUNK
    n_chunks = tile // c
    last_row = 0 if reverse else c - 1

    def rows_of(ci):
        return pl.ds(pl.multiple_of(ci * c, c), c)

    def scale_pass(ci, carry):
        rows = rows_of(ci)
        q = proj_ref[rows, q_off:q_off + HG_WIDTH]
        z = proj_ref[rows, f_off:f_off + HG_WIDTH]
        v = proj_ref[rows, i_off:i_off + HG_WIDTH]
        sig = _sigmoid(z)
        f = lb + (1.0 - lb) * sig
        k = (1.0 - lb) * (1.0 - sig)
        lf2 = jnp.log(f) * LOG2E
        hi = lf2.astype(BF16)
        lo = (lf2 - hi.astype(F32)).astype(BF16)
        parts = jnp.concatenate([hi, lo], axis=0)
        e = jnp.dot(sums_ref[...], parts, preferred_element_type=F32)
        cum = e[0:c]
        tot = cum[last_row:last_row + 1]
        qs_ref[rows, :] = q * jnp.exp2(cum)
        ks_ref[rows, :] = k * jnp.exp2(tot - cum)
        d_ref[ci] = jnp.exp2(tot)
        row = lax.broadcasted_iota(jnp.int32, (c, HG_WIDTH), 0)
        for l, m in enumerate(LEVELS + (1,)):
            if m >= SUBLANES:
                qk = jnp.concatenate(
                    [(q if _is_query_side(r0 // m, reverse) else k)[r0:r0 + m]
                     for r0 in range(0, c, m)], axis=0)
            else:
                qk = jnp.where(_is_query_side(row // m, reverse), q * f if m == 1 else q, k)
            if m > 1:
                qk = qk * jnp.exp2(e[(1 + l) * c:(2 + l) * c])
            xl_ref[l, rows, :] = qk.astype(BF16)
        for h in range(HG_HEADS):
            cols = slice(h * HG_DIM, (h + 1) * HG_DIM)
            diag = jnp.sum(q[:, cols] * k[:, cols], axis=-1, keepdims=True)
            o_ref[rows, cols] = diag * v[:, cols]
        return carry

    def score_pass(ci, carry):
        rows = rows_of(ci)
        for h in range(HG_HEADS):
            cols = slice(h * HG_DIM, (h + 1) * HG_DIM)
            scores = jnp.zeros((c, c), F32)
            for l in range(N_LEVELS):
                x = xl_ref[l, rows, cols]
                a = lax.dot_general(x, x, NT_DIMS, preferred_element_type=F32)
                scores = scores + a * mask_ref[l * c:(l + 1) * c, :]
            sc_ref[ci, h] = scores
        return carry

    def state_pass(i, carry):
        ci = (n_chunks - 1 - i) if reverse else i
        rows = rows_of(ci)
        d = d_ref[ci]
        for h in range(HG_HEADS):
            cols = slice(h * HG_DIM, (h + 1) * HG_DIM)
            v = proj_ref[rows, i_off + h * HG_DIM:i_off + (h + 1) * HG_DIM]
            st = st_ref[h]
            o = (lax.dot_general(qs_ref[rows, cols], st, NT_DIMS, preferred_element_type=F32)
                 + jnp.dot(sc_ref[ci, h], v, preferred_element_type=F32))
            o_ref[rows, cols] += o
            st_ref[h] = st * d[:, cols] + lax.dot_general(v, ks_ref[rows, cols], TN_DIMS,
                                                          preferred_element_type=F32)
        return carry

    lax.fori_loop(0, n_chunks, scale_pass, 0, unroll=True)
    lax.fori_loop(0, n_chunks, score_pass, 0, unroll=True)
    lax.fori_loop(0, n_chunks, state_pass, 0, unroll=True)


def _hgrn_scratch(tile):
    n_chunks = tile // HG_CHUNK
    return [
        pltpu.VMEM((HG_HEADS, HG_DIM, HG_DIM), F32),
        pltpu.VMEM((N_LEVELS, tile, HG_WIDTH), BF16),
        pltpu.VMEM((tile, HG_WIDTH), F32),
        pltpu.VMEM((tile, HG_WIDTH), F32),
        pltpu.VMEM((n_chunks, 1, HG_WIDTH), F32),
        pltpu.VMEM((n_chunks, HG_HEADS, HG_CHUNK, HG_CHUNK), F32),
    ]


def _bwd_kernel(x_ref, g1_ref, w_ref, logit_ref, sums_ref, mask_ref, o_ref,
                proj_ref, st_ref, xl_ref, qs_ref, ks_ref, d_ref, sc_ref, *, layer, tile):
    @pl.when(pl.program_id(1) == 0)
    def _():
        st_ref[...] = jnp.zeros_like(st_ref)

    h = _rms(x_ref[0], g1_ref[...]).astype(BF16)
    proj_ref[...] = jnp.dot(h, w_ref[...], preferred_element_type=F32)
    lb = _lower_bound(logit_ref[...], layer)
    _hgrn_tile(proj_ref, o_ref.at[0], st_ref, xl_ref, qs_ref, ks_ref, d_ref, sc_ref, lb,
               sums_ref, mask_ref, tile=tile, reverse=True,
               q_off=0, f_off=HG_WIDTH, i_off=2 * HG_WIDTH)


def _mix_kernel(x_ref, obw_ref, g1_ref, w_ref, logit_ref, sums_ref, mask_ref,
                hgn_ref, sgn_g_ref, sgn_b_ref, ws_ref, bs_ref, wout_ref, o_ref,
                proj_ref, ofw_ref, sg_ref, acc_ref, st_ref, xl_ref, qs_ref, ks_ref, d_ref, sc_ref,
                *, layer, tile):
    @pl.when(pl.program_id(1) == 0)
    def _():
        st_ref[...] = jnp.zeros_like(st_ref)

    x = x_ref[0]
    h = _rms(x, g1_ref[...]).astype(BF16)
    g_off, u_off, v_off = 3 * HG_WIDTH, 4 * HG_WIDTH, 4 * HG_WIDTH + SG_WIDTH
    proj_ref[:, u_off:] = jnp.dot(h, w_ref[:, u_off:], preferred_element_type=F32)
    proj_ref[:, :u_off] = jnp.dot(h, w_ref[:, :u_off], preferred_element_type=F32)

    for hd in range(SG_HEADS):
        cols = slice(hd * SG_DIM, (hd + 1) * SG_DIM)
        bias = jnp.broadcast_to(bs_ref[hd], (SG_CHUNK, SG_DIM))
        for pc in range(tile // SG_CHUNK):
            rows = slice(pc * SG_CHUNK, (pc + 1) * SG_CHUNK)
            vv = _gelu_tanh(proj_ref[rows, v_off + hd * SG_DIM:v_off + (hd + 1) * SG_DIM])
            mu = jnp.mean(vv, axis=-1, keepdims=True)
            vc = vv - mu
            vn = vc * lax.rsqrt(jnp.mean(vc * vc, axis=-1, keepdims=True) + EPS)
            vn = vn * sgn_g_ref[:, cols] + sgn_b_ref[:, cols]
            sp = jnp.dot(ws_ref[hd], vn, preferred_element_type=F32) + bias
            u = _gelu_tanh(proj_ref[rows, u_off + hd * SG_DIM:u_off + (hd + 1) * SG_DIM])
            sg_ref[rows, cols] = (u * sp).astype(BF16)
    acc_ref[...] = x + jnp.dot(sg_ref[...], wout_ref[HG_WIDTH:, :], preferred_element_type=F32)

    lb = _lower_bound(logit_ref[...], layer)
    _hgrn_tile(proj_ref, ofw_ref, st_ref, xl_ref, qs_ref, ks_ref, d_ref, sc_ref, lb,
               sums_ref, mask_ref, tile=tile, reverse=False,
               q_off=0, f_off=HG_WIDTH, i_off=2 * HG_WIDTH)

    for rb in range(tile // OUT_ROWS):
        rows = slice(rb * OUT_ROWS, (rb + 1) * OUT_ROWS)
        heads = []
        for hd in range(HG_HEADS):
            cols = slice(hd * HG_DIM, (hd + 1) * HG_DIM)
            o = _rms(ofw_ref[rows, cols] + obw_ref[0, rows, cols], hgn_ref[:, cols])
            gate = proj_ref[rows, g_off + hd * HG_DIM:g_off + (hd + 1) * HG_DIM]
            heads.append((o * (gate * _sigmoid(gate))).astype(BF16))
        o_ref[0, rows, :] = acc_ref[rows, :] + jnp.dot(
            jnp.concatenate(heads, axis=1), wout_ref[:HG_WIDTH, :], preferred_element_type=F32)


def _ffn_kernel(x_ref, xp_ref, xn_ref, g2_ref, win_ref, cw_ref, cb_ref, wout_ref, gf_ref,
                o_ref, h_ref, a_ref, act_ref, *, tile, final_norm):
    j = pl.program_id(1)
    n = pl.num_programs(1)
    x = x_ref[0]
    g2 = g2_ref[...]
    h_ref[HALO:HALO + tile, :] = _rms(x, g2).astype(BF16)
    h_ref[0:HALO, :] = (_rms(xp_ref[0], g2) * jnp.where(j > 0, 1.0, 0.0)).astype(BF16)
    h_ref[HALO + tile:2 * HALO + tile, :] = (
        _rms(xn_ref[0], g2) * jnp.where(j < n - 1, 1.0, 0.0)).astype(BF16)

    def up_proj(s):
        a = jnp.dot(h_ref[...], win_ref[s], preferred_element_type=F32)
        for cs in range(N_SLABS):
            a_ref[s % 2, cs] = a[:, cs * LANES:(cs + 1) * LANES]

    def conv_gate(s):
        cw = cw_ref[s]
        cb = cb_ref[s]
        ys = []
        for cs in range(N_SLABS):
            cols = slice(cs * LANES, (cs + 1) * LANES)
            ys.append(a_ref[s % 2, cs, pl.ds(HALO - 1, tile), :] * cw[0:1, cols]
                      + a_ref[s % 2, cs, pl.ds(HALO, tile), :] * cw[1:2, cols]
                      + a_ref[s % 2, cs, pl.ds(HALO + 1, tile), :] * cw[2:3, cols]
                      + cb[:, cols])
        half = N_SLABS // 2
        gate = jnp.concatenate(ys[:half], axis=1)
        up = jnp.concatenate(ys[half:], axis=1)
        act_ref[s] = (gate * _sigmoid(gate) * up).astype(BF16)

    up_proj(0)
    for s in range(FFN_STEPS):
        if s + 1 < FFN_STEPS:
            up_proj(s + 1)
        conv_gate(s)

    acc = x
    for s in range(FFN_STEPS):
        acc = acc + jnp.dot(act_ref[s], wout_ref[s], preferred_element_type=F32)
    if final_norm:
        acc = _rms(acc, gf_ref[...])
    o_ref[0] = acc


def _const_spec(shape):
    nd = len(shape)
    return pl.BlockSpec(shape, lambda b, j: (0,) * nd, pipeline_mode=pl.Buffered(1))


def _params():
    return pltpu.CompilerParams(dimension_semantics=("arbitrary", "arbitrary"),
                                vmem_limit_bytes=VMEM_LIMIT)


def _bwd_call(x, g1, w_bwd, logits, tables, *, layer):
    bsz, seq, _ = x.shape
    tile = MIX_TILE
    nt = seq // tile
    consts = (g1, w_bwd, logits) + tuple(tables)
    return pl.pallas_call(
        functools.partial(_bwd_kernel, layer=layer, tile=tile),
        out_shape=jax.ShapeDtypeStruct((bsz, seq, HG_WIDTH), F32),
        grid=(bsz, nt),
        in_specs=[pl.BlockSpec((1, tile, D_MODEL), lambda b, j: (b, nt - 1 - j, 0))]
        + [_const_spec(c.shape) for c in consts],
        out_specs=pl.BlockSpec((1, tile, HG_WIDTH), lambda b, j: (b, nt - 1 - j, 0)),
        scratch_shapes=[pltpu.VMEM((tile, 3 * HG_WIDTH), F32)] + _hgrn_scratch(tile),
        compiler_params=_params(),
        name=f"hgrn_bwd_l{layer}",
    )(x, *consts)


def _mix_call(x, obw, g1, w_fwd, logits, tables, hgn, sgn_g, sgn_b, ws, bs, wout, *, layer):
    bsz, seq, _ = x.shape
    tile = MIX_TILE
    nt = seq // tile
    consts = (g1, w_fwd, logits) + tuple(tables) + (hgn, sgn_g, sgn_b, ws, bs, wout)
    return pl.pallas_call(
        functools.partial(_mix_kernel, layer=layer, tile=tile),
        out_shape=jax.ShapeDtypeStruct((bsz, seq, D_MODEL), F32),
        grid=(bsz, nt),
        in_specs=[
            pl.BlockSpec((1, tile, D_MODEL), lambda b, j: (b, j, 0)),
            pl.BlockSpec((1, tile, HG_WIDTH), lambda b, j: (b, j, 0)),
        ] + [_const_spec(c.shape) for c in consts],
        out_specs=pl.BlockSpec((1, tile, D_MODEL), lambda b, j: (b, j, 0)),
        scratch_shapes=[
            pltpu.VMEM((tile, 4 * HG_WIDTH + 2 * SG_WIDTH), F32),
            pltpu.VMEM((tile, HG_WIDTH), F32),
            pltpu.VMEM((tile, SG_WIDTH), BF16),
            pltpu.VMEM((tile, D_MODEL), F32),
        ] + _hgrn_scratch(tile),
        compiler_params=_params(),
        name=f"mixer_l{layer}",
    )(x, obw, *consts)


def _ffn_call(x, g2, win, cw, cb, wout, gf, *, layer, final_norm):
    bsz, seq, _ = x.shape
    tile = FFN_TILE
    nt = seq // tile
    per = tile // HALO
    last = seq // HALO - 1
    consts = (g2, win, cw, cb, wout, gf)
    return pl.pallas_call(
        functools.partial(_ffn_kernel, tile=tile, final_norm=final_norm),
        out_shape=jax.ShapeDtypeStruct((bsz, seq, D_MODEL), F32),
        grid=(bsz, nt),
        in_specs=[
            pl.BlockSpec((1, tile, D_MODEL), lambda b, j: (b, j, 0)),
            pl.BlockSpec((1, HALO, D_MODEL), lambda b, j: (b, jnp.maximum(j * per - 1, 0), 0)),
            pl.BlockSpec((1, HALO, D_MODEL),
                         lambda b, j: (b, jnp.minimum((j + 1) * per, last), 0)),
        ] + [_const_spec(c.shape) for c in consts],
        out_specs=pl.BlockSpec((1, tile, D_MODEL), lambda b, j: (b, j, 0)),
        scratch_shapes=[
            pltpu.VMEM((tile + 2 * HALO, D_MODEL), BF16),
            pltpu.VMEM((2, N_SLABS, tile + 2 * HALO, LANES), F32),
            pltpu.VMEM((FFN_STEPS, tile, FFN_COLS), BF16),
        ],
        compiler_params=_params(),
        name=f"ffn_l{layer}",
    )(x, x, x, *consts)


def _trunk(x, layers, final_g, tables_fwd, tables_bwd):
    depth = len(layers)
    for l, p in enumerate(layers):
        obw = _bwd_call(x, p["g1"], p["w_bwd"], p["logits_bwd"], tables_bwd, layer=l)
        x = _mix_call(x, obw, p["g1"], p["w_fwd"], p["logits_fwd"], tables_fwd, p["hgn"],
                      p["sgn_g"], p["sgn_b"], p["ws"], p["bs"], p["wout"], layer=l)
        x = _ffn_call(x, p["g2"], p["win"], p["cw"], p["cb"], p["wffn_out"], final_g,
                      layer=l, final_norm=(l == depth - 1))
    return x


def kernel(x_prompt, x_sample, norm1_g, w_in, hg_lb_logits, hg_norm_g, sg_norm_g, sg_norm_b,
           sg_w_spatial, sg_b_spatial, w_out, norm2_g, w_ffn_in, ffn_conv_w, ffn_conv_b,
           w_ffn_out, final_norm_g):
    depth = w_in.shape[0]
    hw = HG_WIDTH
    layers = []
    for l in range(depth):
        wl = w_in[l]
        q, f_fw, f_bw, i_in, rest = (wl[:, 0:hw], wl[:, hw:2 * hw], wl[:, 2 * hw:3 * hw],
                                     wl[:, 3 * hw:4 * hw], wl[:, 4 * hw:])
        gate_w = w_ffn_in[l][:, :D_FF].reshape(D_MODEL, FFN_STEPS, FFN_COLS)
        up_w = w_ffn_in[l][:, D_FF:].reshape(D_MODEL, FFN_STEPS, FFN_COLS)
        cw = ffn_conv_w[l]
        cb = ffn_conv_b[l]
        layers.append(dict(
            g1=norm1_g[l].reshape(1, D_MODEL),
            w_fwd=jnp.concatenate([q, f_fw, i_in, rest], axis=1).astype(BF16),
            w_bwd=jnp.concatenate([q, f_bw, i_in], axis=1).astype(BF16),
            logits_fwd=hg_lb_logits[0].astype(F32),
            logits_bwd=hg_lb_logits[1].astype(F32),
            hgn=hg_norm_g[l].reshape(1, hw),
            sgn_g=sg_norm_g[l].reshape(1, SG_WIDTH),
            sgn_b=sg_norm_b[l].reshape(1, SG_WIDTH),
            ws=sg_w_spatial[l],
            bs=sg_b_spatial[l].reshape(SG_HEADS, SG_CHUNK, 1),
            wout=w_out[l].astype(BF16),
            g2=norm2_g[l].reshape(1, D_MODEL),
            win=jnp.concatenate([gate_w, up_w], axis=2).transpose(1, 0, 2).astype(BF16),
            cw=jnp.concatenate([cw[:, :D_FF].reshape(3, FFN_STEPS, FFN_COLS),
                                cw[:, D_FF:].reshape(3, FFN_STEPS, FFN_COLS)],
                               axis=2).transpose(1, 0, 2),
            cb=jnp.concatenate([cb[:D_FF].reshape(FFN_STEPS, 1, FFN_COLS),
                                cb[D_FF:].reshape(FFN_STEPS, 1, FFN_COLS)], axis=2),
            wffn_out=w_ffn_out[l].reshape(FFN_STEPS, FFN_COLS, D_MODEL).astype(BF16),
        ))
    final_g = final_norm_g.reshape(1, D_MODEL)
    tables_fwd = _hgrn_tables(False)
    tables_bwd = _hgrn_tables(True)
    y_prompt = _trunk(x_prompt, layers, final_g, tables_fwd, tables_bwd)
    y_sample = _trunk(x_sample, layers, final_g, tables_fwd, tables_bwd)
    return (y_prompt, y_sample)
```

```python
import functools

import jax
import jax.numpy as jnp
import numpy as np
from jax import lax
from jax.experimental import pallas as pl
from jax.experimental.pallas import tpu as pltpu

D_MODEL = 1024
HG_HEADS = 4
HG_DIM = 128
HG_WIDTH = HG_HEADS * HG_DIM
HG_CHUNK = 64
SG_HEADS = 4
SG_DIM = 128
SG_WIDTH = SG_HEADS * SG_DIM
SG_CHUNK = 128
D_FF = 2816
EPS = 1e-6
LOG2E = 1.4426950408889634

LANES = 128
SUBLANES = 8

LEVELS = (32, 16, 8, 4, 2)
N_LEVELS = len(LEVELS) + 1

FFN_COLS = 256
FFN_STEPS = D_FF // FFN_COLS
N_SLABS = 2 * FFN_COLS // LANES
HALO = 16

MIX_TILE = 512
OUT_ROWS = 128
FFN_TILE = 512
VMEM_LIMIT = 56 * 1024 * 1024

F32 = jnp.float32
BF16 = jnp.bfloat16

NT_DIMS = (((1,), (1,)), ((), ()))
TN_DIMS = (((0,), (0,)), ((), ()))


def _is_query_side(block_index, reverse):
    return ((block_index % 2) == 1) != reverse


def _hgrn_tables(reverse):
    c = HG_CHUNK
    r = np.arange(c)
    rr, tt = r[None, :], r[:, None]
    mats = [rr <= tt]
    masks = []
    for m in LEVELS + (1,):
        blk = r // m
        odd = (blk % 2) == 1
        ref = np.where(odd, blk * m - 1, (blk + 1) * m - 1)[:, None]
        if m > 1:
            mats.append(np.where(odd[:, None], (rr > ref) & (rr <= tt), (rr > tt) & (rr <= ref)))
        masks.append(odd[:, None] & (blk[None, :] == blk[:, None] - 1))
    if reverse:
        mats = [a[::-1, ::-1] for a in mats]
        masks = [a[::-1, ::-1] for a in masks]
    sums = np.concatenate(mats, axis=0).astype(np.float32)
    sums2 = np.concatenate([sums, sums], axis=1)
    return (jnp.asarray(sums2, BF16), jnp.asarray(np.concatenate(masks, axis=0), F32))


def _rms(x, g):
    return x * lax.rsqrt(jnp.mean(x * x, axis=-1, keepdims=True) + EPS) * g


def _sigmoid(x):
    return 1.0 / (1.0 + jnp.exp(-x))


def _gelu_tanh(x):
    return 0.5 * x * (1.0 + jnp.tanh(0.7978845608028654 * (x + 0.044715 * (x * x * x))))


def _lower_bound(logits, layer):
    mx = jnp.max(logits, axis=0, keepdims=True)
    e = jnp.exp(logits - mx)
    p = e / jnp.sum(e, axis=0, keepdims=True)
    lb = jnp.zeros((1, HG_WIDTH), F32)
    for j in range(1, layer + 1):
        lb = lb + p[j:j + 1, :]
    return lb


def _hgrn_tile(qi_ref, f_ref, o_ref, st_ref, xl_ref, qs_ref, ks_ref, d_ref, sc_ref, lb,
               sums_ref, mask_ref, *, tile, reverse, f_off):
    c = HG_CHUNK
    n_chunks = tile // c
    last_row = 0 if reverse else c - 1

    def rows_of(ci):
        return pl.ds(pl.multiple_of(ci * c, c), c)

    def scale_pass(ci, carry):
        rows = rows_of(ci)
        q = qi_ref[rows, 0:HG_WIDTH]
        z = f_ref[rows, f_off:f_off + HG_WIDTH]
        v = qi_ref[rows, HG_WIDTH:2 * HG_WIDTH]
        sig = _sigmoid(z)
        f = lb + (1.0 - lb) * sig
        k = (1.0 - lb) * (1.0 - sig)
        lf2 = jnp.log(f) * LOG2E
        hi = lf2.astype(BF16)
        lo = (lf2 - hi.astype(F32)).astype(BF16)
        parts = jnp.concatenate([hi, lo], axis=0)
        e = jnp.dot(sums_ref[...], parts, preferred_element_type=F32)
        cum = e[0:c]
        tot = cum[last_row:last_row + 1]
        qs_ref[rows, :] = q * jnp.exp2(cum)
        ks_ref[rows, :] = k * jnp.exp2(tot - cum)
        d_ref[ci] = jnp.exp2(tot)
        row = lax.broadcasted_iota(jnp.int32, (c, HG_WIDTH), 0)
        for l, m in enumerate(LEVELS + (1,)):
            if m >= SUBLANES:
                qk = jnp.concatenate(
                    [(q if _is_query_side(r0 // m, reverse) else k)[r0:r0 + m]
                     for r0 in range(0, c, m)], axis=0)
            else:
                qk = jnp.where(_is_query_side(row // m, reverse), q * f if m == 1 else q, k)
            if m > 1:
                qk = qk * jnp.exp2(e[(1 + l) * c:(2 + l) * c])
            xl_ref[l, rows, :] = qk.astype(BF16)
        for h in range(HG_HEADS):
            cols = slice(h * HG_DIM, (h + 1) * HG_DIM)
            diag = jnp.sum(q[:, cols] * k[:, cols], axis=-1, keepdims=True)
            o_ref[rows, cols] = diag * v[:, cols]
        return carry

    def score_pass(ci, carry):
        rows = rows_of(ci)
        for h in range(HG_HEADS):
            cols = slice(h * HG_DIM, (h + 1) * HG_DIM)
            scores = jnp.zeros((c, c), F32)
            for l in range(N_LEVELS):
                x = xl_ref[l, rows, cols]
                a = lax.dot_general(x, x, NT_DIMS, preferred_element_type=F32)
                scores = scores + a * mask_ref[l * c:(l + 1) * c, :]
            sc_ref[ci, h] = scores
        return carry

    def state_pass(i, carry):
        ci = (n_chunks - 1 - i) if reverse else i
        rows = rows_of(ci)
        d = d_ref[ci]
        for h in range(HG_HEADS):
            cols = slice(h * HG_DIM, (h + 1) * HG_DIM)
            v = qi_ref[rows, HG_WIDTH + h * HG_DIM:HG_WIDTH + (h + 1) * HG_DIM]
            st = st_ref[h]
            o = (lax.dot_general(qs_ref[rows, cols], st, NT_DIMS, preferred_element_type=F32)
                 + jnp.dot(sc_ref[ci, h], v, preferred_element_type=F32))
            o_ref[rows, cols] += o
            st_ref[h] = st * d[:, cols] + lax.dot_general(v, ks_ref[rows, cols], TN_DIMS,
                                                          preferred_element_type=F32)
        return carry

    lax.fori_loop(0, n_chunks, scale_pass, 0, unroll=True)
    lax.fori_loop(0, n_chunks, score_pass, 0, unroll=True)
    lax.fori_loop(0, n_chunks, state_pass, 0, unroll=True)


def _hgrn_scratch(tile):
    n_chunks = tile // HG_CHUNK
    return [
        pltpu.VMEM((HG_HEADS, HG_DIM, HG_DIM), F32),
        pltpu.VMEM((N_LEVELS, tile, HG_WIDTH), BF16),
        pltpu.VMEM((tile, HG_WIDTH), F32),
        pltpu.VMEM((tile, HG_WIDTH), F32),
        pltpu.VMEM((n_chunks, 1, HG_WIDTH), F32),
        pltpu.VMEM((n_chunks, HG_HEADS, HG_CHUNK, HG_CHUNK), F32),
    ]


def _bwd_kernel(x_ref, g1_ref, w_ref, logit_ref, sums_ref, mask_ref, o_ref, qi_ref,
                proj_ref, st_ref, xl_ref, qs_ref, ks_ref, d_ref, sc_ref, *, layer, tile):
    @pl.when(pl.program_id(1) == 0)
    def _():
        st_ref[...] = jnp.zeros_like(st_ref)

    for rows in (slice(0, tile // 2), slice(tile // 2, tile)):
        h = _rms(x_ref[0, rows, :], g1_ref[...]).astype(BF16)
        p = jnp.dot(h, w_ref[...], preferred_element_type=F32)
        qi_ref[0, rows, :] = p[:, :2 * HG_WIDTH]
        proj_ref[rows, :] = p[:, 2 * HG_WIDTH:]
    lb = _lower_bound(logit_ref[...], layer)
    _hgrn_tile(qi_ref.at[0], proj_ref, o_ref.at[0], st_ref, xl_ref, qs_ref, ks_ref, d_ref,
               sc_ref, lb, sums_ref, mask_ref, tile=tile, reverse=True, f_off=0)


def _mix_kernel(x_ref, obw_ref, qi_ref, g1_ref, w_ref, logit_ref, sums_ref, mask_ref,
                hgn_ref, sgn_g_ref, sgn_b_ref, ws_ref, bs_ref, wout_ref, o_ref,
                proj_ref, ofw_ref, sg_ref, acc_ref, st_ref, xl_ref, qs_ref, ks_ref, d_ref, sc_ref,
                *, layer, tile):
    @pl.when(pl.program_id(1) == 0)
    def _():
        st_ref[...] = jnp.zeros_like(st_ref)

    x = x_ref[0]
    h = _rms(x, g1_ref[...]).astype(BF16)
    g_off, u_off, v_off = HG_WIDTH, 2 * HG_WIDTH, 2 * HG_WIDTH + SG_WIDTH
    for rows in (slice(0, tile // 2), slice(tile // 2, tile)):
        proj_ref[rows, u_off:] = jnp.dot(h[rows], w_ref[:, u_off:], preferred_element_type=F32)
    proj_ref[:, :u_off] = jnp.dot(h, w_ref[:, :u_off], preferred_element_type=F32)

    for hd in range(SG_HEADS):
        cols = slice(hd * SG_DIM, (hd + 1) * SG_DIM)
        bias = jnp.broadcast_to(bs_ref[hd], (SG_CHUNK, SG_DIM))
        for pc in range(tile // SG_CHUNK):
            rows = slice(pc * SG_CHUNK, (pc + 1) * SG_CHUNK)
            vv = _gelu_tanh(proj_ref[rows, v_off + hd * SG_DIM:v_off + (hd + 1) * SG_DIM])
            mu = jnp.mean(vv, axis=-1, keepdims=True)
            vc = vv - mu
            vn = vc * lax.rsqrt(jnp.mean(vc * vc, axis=-1, keepdims=True) + EPS)
            vn = vn * sgn_g_ref[:, cols] + sgn_b_ref[:, cols]
            sp = jnp.dot(ws_ref[hd], vn, preferred_element_type=F32) + bias
            u = _gelu_tanh(proj_ref[rows, u_off + hd * SG_DIM:u_off + (hd + 1) * SG_DIM])
            sg_ref[rows, cols] = (u * sp).astype(BF16)
    acc_ref[...] = x + jnp.dot(sg_ref[...], wout_ref[HG_WIDTH:, :], preferred_element_type=F32)

    lb = _lower_bound(logit_ref[...], layer)
    _hgrn_tile(qi_ref.at[0], proj_ref, ofw_ref, st_ref, xl_ref, qs_ref, ks_ref, d_ref, sc_ref,
               lb, sums_ref, mask_ref, tile=tile, reverse=False, f_off=0)

    for rb in range(tile // OUT_ROWS):
        rows = slice(rb * OUT_ROWS, (rb + 1) * OUT_ROWS)
        heads = []
        for hd in range(HG_HEADS):
            cols = slice(hd * HG_DIM, (hd + 1) * HG_DIM)
            o = _rms(ofw_ref[rows, cols] + obw_ref[0, rows, cols], hgn_ref[:, cols])
            gate = proj_ref[rows, g_off + hd * HG_DIM:g_off + (hd + 1) * HG_DIM]
            heads.append((o * (gate * _sigmoid(gate))).astype(BF16))
        o_ref[0, rows, :] = acc_ref[rows, :] + jnp.dot(
            jnp.concatenate(heads, axis=1), wout_ref[:HG_WIDTH, :], preferred_element_type=F32)


def _ffn_kernel(x_ref, xp_ref, xn_ref, g2_ref, win_ref, cw_ref, cb_ref, wout_ref, gf_ref,
                o_ref, h_ref, a_ref, act_ref, *, tile, final_norm):
    j = pl.program_id(1)
    n = pl.num_programs(1)
    x = x_ref[0]
    g2 = g2_ref[...]
    h_ref[HALO:HALO + tile, :] = _rms(x, g2).astype(BF16)
    h_ref[0:HALO, :] = (_rms(xp_ref[0], g2) * jnp.where(j > 0, 1.0, 0.0)).astype(BF16)
    h_ref[HALO + tile:2 * HALO + tile, :] = (
        _rms(xn_ref[0], g2) * jnp.where(j < n - 1, 1.0, 0.0)).astype(BF16)

    rows_all = tile + 2 * HALO

    def up_proj(s):
        n_parts = 2 if s == 0 else 1
        part = rows_all // n_parts
        for p in range(n_parts):
            rows = slice(p * part, (p + 1) * part)
            a = jnp.dot(h_ref[rows, :], win_ref[s], preferred_element_type=F32)
            for cs in range(N_SLABS):
                a_ref[s, cs, rows, :] = a[:, cs * LANES:(cs + 1) * LANES]

    def conv_gate(s):
        cw = cw_ref[s]
        cb = cb_ref[s]
        ys = []
        for cs in range(N_SLABS):
            cols = slice(cs * LANES, (cs + 1) * LANES)
            ys.append(a_ref[s, cs, pl.ds(HALO - 1, tile), :] * cw[0:1, cols]
                      + a_ref[s, cs, pl.ds(HALO, tile), :] * cw[1:2, cols]
                      + a_ref[s, cs, pl.ds(HALO + 1, tile), :] * cw[2:3, cols]
                      + cb[:, cols])
        half = N_SLABS // 2
        gate = jnp.concatenate(ys[:half], axis=1)
        up = jnp.concatenate(ys[half:], axis=1)
        act_ref[s] = (gate * _sigmoid(gate) * up).astype(BF16)

    up_proj(0)
    for s in range(FFN_STEPS):
        if s + 1 < FFN_STEPS:
            up_proj(s + 1)
        conv_gate(s)

    acc = x
    for s in range(FFN_STEPS):
        acc = acc + jnp.dot(act_ref[s], wout_ref[s], preferred_element_type=F32)
    if final_norm:
        acc = _rms(acc, gf_ref[...])
    o_ref[0] = acc


def _const_spec(shape):
    nd = len(shape)
    return pl.BlockSpec(shape, lambda b, j: (0,) * nd, pipeline_mode=pl.Buffered(1))


def _params():
    return pltpu.CompilerParams(dimension_semantics=("arbitrary", "arbitrary"),
                                vmem_limit_bytes=VMEM_LIMIT)


def _bwd_call(x, g1, w_bwd, logits, tables, *, layer):
    bsz, seq, _ = x.shape
    tile = MIX_TILE
    nt = seq // tile
    consts = (g1, w_bwd, logits) + tuple(tables)

    def rev(b, j):
        return (b, nt - 1 - j, 0)

    return pl.pallas_call(
        functools.partial(_bwd_kernel, layer=layer, tile=tile),
        out_shape=(jax.ShapeDtypeStruct((bsz, seq, HG_WIDTH), F32),
                   jax.ShapeDtypeStruct((bsz, seq, 2 * HG_WIDTH), F32)),
        grid=(bsz, nt),
        in_specs=[pl.BlockSpec((1, tile, D_MODEL), rev)]
        + [_const_spec(c.shape) for c in consts],
        out_specs=(pl.BlockSpec((1, tile, HG_WIDTH), rev),
                   pl.BlockSpec((1, tile, 2 * HG_WIDTH), rev)),
        scratch_shapes=[pltpu.VMEM((tile, HG_WIDTH), F32)] + _hgrn_scratch(tile),
        compiler_params=_params(),
        name=f"hgrn_bwd_l{layer}",
    )(x, *consts)


def _mix_call(x, obw, qi, g1, w_fwd, logits, tables, hgn, sgn_g, sgn_b, ws, bs, wout, *, layer):
    bsz, seq, _ = x.shape
    tile = MIX_TILE
    nt = seq // tile
    consts = (g1, w_fwd, logits) + tuple(tables) + (hgn, sgn_g, sgn_b, ws, bs, wout)
    return pl.pallas_call(
        functools.partial(_mix_kernel, layer=layer, tile=tile),
        out_shape=jax.ShapeDtypeStruct((bsz, seq, D_MODEL), F32),
        grid=(bsz, nt),
        in_specs=[
            pl.BlockSpec((1, tile, D_MODEL), lambda b, j: (b, j, 0)),
            pl.BlockSpec((1, tile, HG_WIDTH), lambda b, j: (b, j, 0)),
            pl.BlockSpec((1, tile, 2 * HG_WIDTH), lambda b, j: (b, j, 0)),
        ] + [_const_spec(c.shape) for c in consts],
        out_specs=pl.BlockSpec((1, tile, D_MODEL), lambda b, j: (b, j, 0)),
        scratch_shapes=[
            pltpu.VMEM((tile, 2 * HG_WIDTH + 2 * SG_WIDTH), F32),
            pltpu.VMEM((tile, HG_WIDTH), F32),
            pltpu.VMEM((tile, SG_WIDTH), BF16),
            pltpu.VMEM((tile, D_MODEL), F32),
        ] + _hgrn_scratch(tile),
        compiler_params=_params(),
        name=f"mixer_l{layer}",
    )(x, obw, qi, *consts)


def _ffn_call(x, g2, win, cw, cb, wout, gf, *, layer, final_norm):
    bsz, seq, _ = x.shape
    tile = FFN_TILE
    nt = seq // tile
    per = tile // HALO
    last = seq // HALO - 1
    consts = (g2, win, cw, cb, wout, gf)
    return pl.pallas_call(
        functools.partial(_ffn_kernel, tile=tile, final_norm=final_norm),
        out_shape=jax.ShapeDtypeStruct((bsz, seq, D_MODEL), F32),
        grid=(bsz, nt),
        in_specs=[
            pl.BlockSpec((1, tile, D_MODEL), lambda b, j: (b, j, 0)),
            pl.BlockSpec((1, HALO, D_MODEL), lambda b, j: (b, jnp.maximum(j * per - 1, 0), 0)),
            pl.BlockSpec((1, HALO, D_MODEL),
                         lambda b, j: (b, jnp.minimum((j + 1) * per, last), 0)),
        ] + [_const_spec(c.shape) for c in consts],
        out_specs=pl.BlockSpec((1, tile, D_MODEL), lambda b, j: (b, j, 0)),
        scratch_shapes=[
            pltpu.VMEM((tile + 2 * HALO, D_MODEL), BF16),
            pltpu.VMEM((FFN_STEPS, N_SLABS, tile + 2 * HALO, LANES), F32),
            pltpu.VMEM((FFN_STEPS, tile, FFN_COLS), BF16),
        ],
        compiler_params=_params(),
        name=f"ffn_l{layer}",
    )(x, x, x, *consts)


def _trunk(x, layers, final_g, tables_fwd, tables_bwd):
    depth = len(layers)
    for l, p in enumerate(layers):
        obw, qi = _bwd_call(x, p["g1"], p["w_bwd"], p["logits_bwd"], tables_bwd, layer=l)
        x = _mix_call(x, obw, qi, p["g1"], p["w_fwd"], p["logits_fwd"], tables_fwd, p["hgn"],
                      p["sgn_g"], p["sgn_b"], p["ws"], p["bs"], p["wout"], layer=l)
        x = _ffn_call(x, p["g2"], p["win"], p["cw"], p["cb"], p["wffn_out"], final_g,
                      layer=l, final_norm=(l == depth - 1))
    return x


def kernel(x_prompt, x_sample, norm1_g, w_in, hg_lb_logits, hg_norm_g, sg_norm_g, sg_norm_b,
           sg_w_spatial, sg_b_spatial, w_out, norm2_g, w_ffn_in, ffn_conv_w, ffn_conv_b,
           w_ffn_out, final_norm_g):
    depth = w_in.shape[0]
    hw = HG_WIDTH
    layers = []
    for l in range(depth):
        wl = w_in[l]
        q, f_fw, f_bw, i_in, rest = (wl[:, 0:hw], wl[:, hw:2 * hw], wl[:, 2 * hw:3 * hw],
                                     wl[:, 3 * hw:4 * hw], wl[:, 4 * hw:])
        gate_w = w_ffn_in[l][:, :D_FF].reshape(D_MODEL, FFN_STEPS, FFN_COLS)
        up_w = w_ffn_in[l][:, D_FF:].reshape(D_MODEL, FFN_STEPS, FFN_COLS)
        cw = ffn_conv_w[l]
        cb = ffn_conv_b[l]
        layers.append(dict(
            g1=norm1_g[l].reshape(1, D_MODEL),
            w_fwd=jnp.concatenate([f_fw, rest], axis=1).astype(BF16),
            w_bwd=jnp.concatenate([q, i_in, f_bw], axis=1).astype(BF16),
            logits_fwd=hg_lb_logits[0].astype(F32),
            logits_bwd=hg_lb_logits[1].astype(F32),
            hgn=hg_norm_g[l].reshape(1, hw),
            sgn_g=sg_norm_g[l].reshape(1, SG_WIDTH),
            sgn_b=sg_norm_b[l].reshape(1, SG_WIDTH),
            ws=sg_w_spatial[l],
            bs=sg_b_spatial[l].reshape(SG_HEADS, SG_CHUNK, 1),
            wout=w_out[l].astype(BF16),
            g2=norm2_g[l].reshape(1, D_MODEL),
            win=jnp.concatenate([gate_w, up_w], axis=2).transpose(1, 0, 2).astype(BF16),
            cw=jnp.concatenate([cw[:, :D_FF].reshape(3, FFN_STEPS, FFN_COLS),
                                cw[:, D_FF:].reshape(3, FFN_STEPS, FFN_COLS)],
                               axis=2).transpose(1, 0, 2),
            cb=jnp.concatenate([cb[:D_FF].reshape(FFN_STEPS, 1, FFN_COLS),
                                cb[D_FF:].reshape(FFN_STEPS, 1, FFN_COLS)], axis=2),
            wffn_out=w_ffn_out[l].reshape(FFN_STEPS, FFN_COLS, D_MODEL).astype(BF16),
        ))
    final_g = final_norm_g.reshape(1, D_MODEL)
    tables_fwd = _hgrn_tables(False)
    tables_bwd = _hgrn_tables(True)
    y_prompt = _trunk(x_prompt, layers, final_g, tables_fwd, tables_bwd)
    y_sample = _trunk(x_sample, layers, final_g, tables_fwd, tables_bwd)
    return (y_prompt, y_sample)
```

```python
import functools

import jax
import jax.numpy as jnp
import numpy as np
from jax import lax
from jax.experimental import pallas as pl
from jax.experimental.pallas import tpu as pltpu

D_MODEL = 1024
HG_HEADS = 4
HG_DIM = 128
HG_WIDTH = HG_HEADS * HG_DIM
HG_CHUNK = 64
SG_HEADS = 4
SG_DIM = 128
SG_WIDTH = SG_HEADS * SG_DIM
SG_CHUNK = 128
D_FF = 2816
EPS = 1e-6
LOG2E = 1.4426950408889634

LANES = 128
SUBLANES = 8

LEVELS = (32, 16, 8, 4, 2)
SUMMED_LEVELS = tuple(m for m in LEVELS if m < 8)
N_LEVELS = len(LEVELS) + 1

FFN_COLS = 256
FFN_STEPS = D_FF // FFN_COLS
N_SLABS = 2 * FFN_COLS // LANES
HALO = 16

MIX_TILE = 512
OUT_ROWS = 128
FFN_TILE = 512
VMEM_LIMIT = 56 * 1024 * 1024

F32 = jnp.float32
BF16 = jnp.bfloat16

NT_DIMS = (((1,), (1,)), ((), ()))
TN_DIMS = (((0,), (0,)), ((), ()))


def _is_query_side(block_index, reverse):
    return ((block_index % 2) == 1) != reverse


def _hgrn_tables(reverse):
    c = HG_CHUNK
    r = np.arange(c)
    rr, tt = r[None, :], r[:, None]
    mats = [rr <= tt]
    masks = []
    for m in LEVELS + (1,):
        blk = r // m
        odd = (blk % 2) == 1
        ref = np.where(odd, blk * m - 1, (blk + 1) * m - 1)[:, None]
        if m in SUMMED_LEVELS:
            mats.append(np.where(odd[:, None], (rr > ref) & (rr <= tt), (rr > tt) & (rr <= ref)))
        masks.append(odd[:, None] & (blk[None, :] == blk[:, None] - 1))
    if reverse:
        mats = [a[::-1, ::-1] for a in mats]
        masks = [a[::-1, ::-1] for a in masks]
    sums = np.concatenate(mats, axis=0).astype(np.float32)
    sums2 = np.concatenate([sums, sums], axis=1)
    return (jnp.asarray(sums2, BF16), jnp.asarray(np.concatenate(masks, axis=0), F32))


def _rms(x, g):
    return x * lax.rsqrt(jnp.mean(x * x, axis=-1, keepdims=True) + EPS) * g


def _sigmoid(x):
    return 1.0 / (1.0 + jnp.exp(-x))


def _gelu_tanh(x):
    return 0.5 * x * (1.0 + jnp.tanh(0.7978845608028654 * (x + 0.044715 * (x * x * x))))


def _lower_bound(logits, layer):
    mx = jnp.max(logits, axis=0, keepdims=True)
    e = jnp.exp(logits - mx)
    p = e / jnp.sum(e, axis=0, keepdims=True)
    lb = jnp.zeros((1, HG_WIDTH), F32)
    for j in range(1, layer + 1):
        lb = lb + p[j:j + 1, :]
    return lb


def _hgrn_tile(qi_ref, f_ref, o_ref, st_ref, xl_ref, qs_ref, ks_ref, d_ref, sc_ref, lb,
               sums_ref, mask_ref, *, tile, reverse, f_off):
    c = HG_CHUNK
    n_chunks = tile // c
    last_row = 0 if reverse else c - 1

    def rows_of(ci):
        return pl.ds(pl.multiple_of(ci * c, c), c)

    def scale_pass(ci, carry):
        rows = rows_of(ci)
        q = qi_ref[rows, 0:HG_WIDTH]
        z = f_ref[rows, f_off:f_off + HG_WIDTH]
        v = qi_ref[rows, HG_WIDTH:2 * HG_WIDTH]
        sig = _sigmoid(z)
        f = lb + (1.0 - lb) * sig
        k = (1.0 - lb) * (1.0 - sig)
        lf2 = jnp.log(f) * LOG2E
        hi = lf2.astype(BF16)
        lo = (lf2 - hi.astype(F32)).astype(BF16)
        parts = jnp.concatenate([hi, lo], axis=0)
        e = jnp.dot(sums_ref[...], parts, preferred_element_type=F32)
        cum = e[0:c]
        tot = cum[last_row:last_row + 1]
        qs_ref[rows, :] = q * jnp.exp2(cum)
        ks_ref[rows, :] = k * jnp.exp2(tot - cum)
        d_ref[ci] = jnp.exp2(tot)
        row = lax.broadcasted_iota(jnp.int32, (c, HG_WIDTH), 0)
        for l, m in enumerate(LEVELS + (1,)):
            if m >= SUBLANES:
                blocks = []
                for r0 in range(0, c, m):
                    blk = slice(r0, r0 + m)
                    if _is_query_side(r0 // m, reverse):
                        ref = r0 + m if reverse else r0 - 1
                        blocks.append(q[blk] * jnp.exp2(cum[blk] - cum[ref:ref + 1]))
                    else:
                        ref = r0 if reverse else r0 + m - 1
                        blocks.append(k[blk] * jnp.exp2(cum[ref:ref + 1] - cum[blk]))
                qk = jnp.concatenate(blocks, axis=0)
            else:
                qk = jnp.where(_is_query_side(row // m, reverse), q * f if m == 1 else q, k)
                if m > 1:
                    i = 1 + SUMMED_LEVELS.index(m)
                    qk = qk * jnp.exp2(e[i * c:(i + 1) * c])
            xl_ref[l, rows, :] = qk.astype(BF16)
        for h in range(HG_HEADS):
            cols = slice(h * HG_DIM, (h + 1) * HG_DIM)
            diag = jnp.sum(q[:, cols] * k[:, cols], axis=-1, keepdims=True)
            o_ref[rows, cols] = diag * v[:, cols]
        return carry

    def score_pass(ci, carry):
        rows = rows_of(ci)
        for h in range(HG_HEADS):
            cols = slice(h * HG_DIM, (h + 1) * HG_DIM)
            scores = jnp.zeros((c, c), F32)
            for l in range(N_LEVELS):
                x = xl_ref[l, rows, cols]
                a = lax.dot_general(x, x, NT_DIMS, preferred_element_type=F32)
                scores = scores + a * mask_ref[l * c:(l + 1) * c, :]
            sc_ref[ci, h] = scores
        return carry

    def state_pass(i, carry):
        ci = (n_chunks - 1 - i) if reverse else i
        rows = rows_of(ci)
        d = d_ref[ci]
        for h in range(HG_HEADS):
            cols = slice(h * HG_DIM, (h + 1) * HG_DIM)
            v = qi_ref[rows, HG_WIDTH + h * HG_DIM:HG_WIDTH + (h + 1) * HG_DIM]
            st = st_ref[h]
            o = (lax.dot_general(qs_ref[rows, cols], st, NT_DIMS, preferred_element_type=F32)
                 + jnp.dot(sc_ref[ci, h], v, preferred_element_type=F32))
            o_ref[rows, cols] += o
            st_ref[h] = st * d[:, cols] + lax.dot_general(v, ks_ref[rows, cols], TN_DIMS,
                                                          preferred_element_type=F32)
        return carry

    lax.fori_loop(0, n_chunks, scale_pass, 0, unroll=True)
    lax.fori_loop(0, n_chunks, score_pass, 0, unroll=True)
    lax.fori_loop(0, n_chunks, state_pass, 0, unroll=True)


def _hgrn_scratch(tile):
    n_chunks = tile // HG_CHUNK
    return [
        pltpu.VMEM((HG_HEADS, HG_DIM, HG_DIM), F32),
        pltpu.VMEM((N_LEVELS, tile, HG_WIDTH), BF16),
        pltpu.VMEM((tile, HG_WIDTH), F32),
        pltpu.VMEM((tile, HG_WIDTH), F32),
        pltpu.VMEM((n_chunks, 1, HG_WIDTH), F32),
        pltpu.VMEM((n_chunks, HG_HEADS, HG_CHUNK, HG_CHUNK), F32),
    ]


def _bwd_kernel(x_ref, g1_ref, w_ref, logit_ref, sums_ref, mask_ref, o_ref, qi_ref,
                proj_ref, st_ref, xl_ref, qs_ref, ks_ref, d_ref, sc_ref, *, layer, tile):
    @pl.when(pl.program_id(1) == 0)
    def _():
        st_ref[...] = jnp.zeros_like(st_ref)

    for rows in (slice(0, tile // 2), slice(tile // 2, tile)):
        h = _rms(x_ref[0, rows, :], g1_ref[...]).astype(BF16)
        p = jnp.dot(h, w_ref[...], preferred_element_type=F32)
        qi_ref[0, rows, :] = p[:, :2 * HG_WIDTH]
        proj_ref[rows, :] = p[:, 2 * HG_WIDTH:]
    lb = _lower_bound(logit_ref[...], layer)
    _hgrn_tile(qi_ref.at[0], proj_ref, o_ref.at[0], st_ref, xl_ref, qs_ref, ks_ref, d_ref,
               sc_ref, lb, sums_ref, mask_ref, tile=tile, reverse=True, f_off=0)


def _mix_kernel(x_ref, obw_ref, qi_ref, g1_ref, w_ref, logit_ref, sums_ref, mask_ref,
                hgn_ref, sgn_g_ref, sgn_b_ref, ws_ref, bs_ref, wout_ref, o_ref,
                proj_ref, ofw_ref, sg_ref, acc_ref, st_ref, xl_ref, qs_ref, ks_ref, d_ref, sc_ref,
                *, layer, tile):
    @pl.when(pl.program_id(1) == 0)
    def _():
        st_ref[...] = jnp.zeros_like(st_ref)

    x = x_ref[0]
    h = _rms(x, g1_ref[...]).astype(BF16)
    g_off, u_off, v_off = HG_WIDTH, 2 * HG_WIDTH, 2 * HG_WIDTH + SG_WIDTH
    for rows in (slice(0, tile // 2), slice(tile // 2, tile)):
        proj_ref[rows, u_off:] = jnp.dot(h[rows], w_ref[:, u_off:], preferred_element_type=F32)
    proj_ref[:, :u_off] = jnp.dot(h, w_ref[:, :u_off], preferred_element_type=F32)

    for hd in range(SG_HEADS):
        cols = slice(hd * SG_DIM, (hd + 1) * SG_DIM)
        bias = jnp.broadcast_to(bs_ref[hd], (SG_CHUNK, SG_DIM))
        for pc in range(tile // SG_CHUNK):
            rows = slice(pc * SG_CHUNK, (pc + 1) * SG_CHUNK)
            vv = _gelu_tanh(proj_ref[rows, v_off + hd * SG_DIM:v_off + (hd + 1) * SG_DIM])
            mu = jnp.mean(vv, axis=-1, keepdims=True)
            vc = vv - mu
            vn = vc * lax.rsqrt(jnp.mean(vc * vc, axis=-1, keepdims=True) + EPS)
            vn = vn * sgn_g_ref[:, cols] + sgn_b_ref[:, cols]
            sp = jnp.dot(ws_ref[hd], vn, preferred_element_type=F32) + bias
            u = _gelu_tanh(proj_ref[rows, u_off + hd * SG_DIM:u_off + (hd + 1) * SG_DIM])
            sg_ref[rows, cols] = (u * sp).astype(BF16)
    acc_ref[...] = x + jnp.dot(sg_ref[...], wout_ref[HG_WIDTH:, :], preferred_element_type=F32)

    lb = _lower_bound(logit_ref[...], layer)
    _hgrn_tile(qi_ref.at[0], proj_ref, ofw_ref, st_ref, xl_ref, qs_ref, ks_ref, d_ref, sc_ref,
               lb, sums_ref, mask_ref, tile=tile, reverse=False, f_off=0)

    for rb in range(tile // OUT_ROWS):
        rows = slice(rb * OUT_ROWS, (rb + 1) * OUT_ROWS)
        heads = []
        for hd in range(HG_HEADS):
            cols = slice(hd * HG_DIM, (hd + 1) * HG_DIM)
            o = _rms(ofw_ref[rows, cols] + obw_ref[0, rows, cols], hgn_ref[:, cols])
            gate = proj_ref[rows, g_off + hd * HG_DIM:g_off + (hd + 1) * HG_DIM]
            heads.append((o * (gate * _sigmoid(gate))).astype(BF16))
        o_ref[0, rows, :] = acc_ref[rows, :] + jnp.dot(
            jnp.concatenate(heads, axis=1), wout_ref[:HG_WIDTH, :], preferred_element_type=F32)


def _ffn_kernel(x_ref, xp_ref, xn_ref, g2_ref, win_ref, cw_ref, cb_ref, wout_ref, gf_ref,
                o_ref, h_ref, a_ref, act_ref, *, tile, final_norm):
    j = pl.program_id(1)
    n = pl.num_programs(1)
    x = x_ref[0]
    g2 = g2_ref[...]
    h_ref[0:tile, :] = _rms(x, g2).astype(BF16)
    hp = _rms(xp_ref[0, HALO - 1:HALO, :], g2) * jnp.where(j > 0, 1.0, 0.0)
    hn = _rms(xn_ref[0, 0:1, :], g2) * jnp.where(j < n - 1, 1.0, 0.0)
    rid = lax.broadcasted_iota(jnp.int32, (HALO, D_MODEL), 0)
    h_ref[tile:tile + HALO, :] = jnp.where(
        rid == 0, hp, jnp.where(rid == 1, hn, 0.0)).astype(BF16)

    def up_proj(s):
        bounds = (0, tile // 2, tile + HALO) if s == 0 else (0, tile + HALO)
        for lo, hi in zip(bounds[:-1], bounds[1:]):
            a = jnp.dot(h_ref[lo:hi, :], win_ref[s], preferred_element_type=F32)
            n_main = min(hi, tile) - lo
            for cs in range(N_SLABS):
                cols = slice(cs * LANES, (cs + 1) * LANES)
                a_ref[s, cs, HALO + lo:HALO + lo + n_main, :] = a[0:n_main, cols]
                if hi > tile:
                    a_ref[s, cs, HALO - 1:HALO, :] = a[n_main:n_main + 1, cols]
                    a_ref[s, cs, HALO + tile:HALO + tile + 1, :] = a[n_main + 1:n_main + 2, cols]

    def conv_gate(s):
        cw = cw_ref[s]
        cb = cb_ref[s]
        ys = []
        for cs in range(N_SLABS):
            cols = slice(cs * LANES, (cs + 1) * LANES)
            ys.append(a_ref[s, cs, pl.ds(HALO - 1, tile), :] * cw[0:1, cols]
                      + a_ref[s, cs, pl.ds(HALO, tile), :] * cw[1:2, cols]
                      + a_ref[s, cs, pl.ds(HALO + 1, tile), :] * cw[2:3, cols]
                      + cb[:, cols])
        half = N_SLABS // 2
        gate = jnp.concatenate(ys[:half], axis=1)
        up = jnp.concatenate(ys[half:], axis=1)
        act_ref[s] = (gate * _sigmoid(gate) * up).astype(BF16)

    up_proj(0)
    for s in range(FFN_STEPS):
        if s + 1 < FFN_STEPS:
            up_proj(s + 1)
        conv_gate(s)

    acc = x
    for s in range(FFN_STEPS):
        acc = acc + jnp.dot(act_ref[s], wout_ref[s], preferred_element_type=F32)
    if final_norm:
        acc = _rms(acc, gf_ref[...])
    o_ref[0] = acc


def _const_spec(shape):
    nd = len(shape)
    return pl.BlockSpec(shape, lambda b, j: (0,) * nd, pipeline_mode=pl.Buffered(1))


def _params():
    return pltpu.CompilerParams(dimension_semantics=("arbitrary", "arbitrary"),
                                vmem_limit_bytes=VMEM_LIMIT)


def _bwd_call(x, g1, w_bwd, logits, tables, *, layer):
    bsz, seq, _ = x.shape
    tile = MIX_TILE
    nt = seq // tile
    consts = (g1, w_bwd, logits) + tuple(tables)

    def rev(b, j):
        return (b, nt - 1 - j, 0)

    return pl.pallas_call(
        functools.partial(_bwd_kernel, layer=layer, tile=tile),
        out_shape=(jax.ShapeDtypeStruct((bsz, seq, HG_WIDTH), F32),
                   jax.ShapeDtypeStruct((bsz, seq, 2 * HG_WIDTH), F32)),
        grid=(bsz, nt),
        in_specs=[pl.BlockSpec((1, tile, D_MODEL), rev)]
        + [_const_spec(c.shape) for c in consts],
        out_specs=(pl.BlockSpec((1, tile, HG_WIDTH), rev),
                   pl.BlockSpec((1, tile, 2 * HG_WIDTH), rev)),
        scratch_shapes=[pltpu.VMEM((tile, HG_WIDTH), F32)] + _hgrn_scratch(tile),
        compiler_params=_params(),
        name=f"hgrn_bwd_l{layer}",
    )(x, *consts)


def _mix_call(x, obw, qi, g1, w_fwd, logits, tables, hgn, sgn_g, sgn_b, ws, bs, wout, *, layer):
    bsz, seq, _ = x.shape
    tile = MIX_TILE
    nt = seq // tile
    consts = (g1, w_fwd, logits) + tuple(tables) + (hgn, sgn_g, sgn_b, ws, bs, wout)
    return pl.pallas_call(
        functools.partial(_mix_kernel, layer=layer, tile=tile),
        out_shape=jax.ShapeDtypeStruct((bsz, seq, D_MODEL), F32),
        grid=(bsz, nt),
        in_specs=[
            pl.BlockSpec((1, tile, D_MODEL), lambda b, j: (b, j, 0)),
            pl.BlockSpec((1, tile, HG_WIDTH), lambda b, j: (b, j, 0)),
            pl.BlockSpec((1, tile, 2 * HG_WIDTH), lambda b, j: (b, j, 0)),
        ] + [_const_spec(c.shape) for c in consts],
        out_specs=pl.BlockSpec((1, tile, D_MODEL), lambda b, j: (b, j, 0)),
        scratch_shapes=[
            pltpu.VMEM((tile, 2 * HG_WIDTH + 2 * SG_WIDTH), F32),
            pltpu.VMEM((tile, HG_WIDTH), F32),
            pltpu.VMEM((tile, SG_WIDTH), BF16),
            pltpu.VMEM((tile, D_MODEL), F32),
        ] + _hgrn_scratch(tile),
        compiler_params=_params(),
        name=f"mixer_l{layer}",
    )(x, obw, qi, *consts)


def _ffn_call(x, g2, win, cw, cb, wout, gf, *, layer, final_norm):
    bsz, seq, _ = x.shape
    tile = FFN_TILE
    nt = seq // tile
    per = tile // HALO
    last = seq // HALO - 1
    consts = (g2, win, cw, cb, wout, gf)
    return pl.pallas_call(
        functools.partial(_ffn_kernel, tile=tile, final_norm=final_norm),
        out_shape=jax.ShapeDtypeStruct((bsz, seq, D_MODEL), F32),
        grid=(bsz, nt),
        in_specs=[
            pl.BlockSpec((1, tile, D_MODEL), lambda b, j: (b, j, 0)),
            pl.BlockSpec((1, HALO, D_MODEL), lambda b, j: (b, jnp.maximum(j * per - 1, 0), 0)),
            pl.BlockSpec((1, HALO, D_MODEL),
                         lambda b, j: (b, jnp.minimum((j + 1) * per, last), 0)),
        ] + [_const_spec(c.shape) for c in consts],
        out_specs=pl.BlockSpec((1, tile, D_MODEL), lambda b, j: (b, j, 0)),
        scratch_shapes=[
            pltpu.VMEM((tile + HALO, D_MODEL), BF16),
            pltpu.VMEM((FFN_STEPS, N_SLABS, tile + 2 * HALO, LANES), F32),
            pltpu.VMEM((FFN_STEPS, tile, FFN_COLS), BF16),
        ],
        compiler_params=_params(),
        name=f"ffn_l{layer}",
    )(x, x, x, *consts)


def _trunk(x, layers, final_g, tables_fwd, tables_bwd):
    depth = len(layers)
    for l, p in enumerate(layers):
        obw, qi = _bwd_call(x, p["g1"], p["w_bwd"], p["logits_bwd"], tables_bwd, layer=l)
        x = _mix_call(x, obw, qi, p["g1"], p["w_fwd"], p["logits_fwd"], tables_fwd, p["hgn"],
                      p["sgn_g"], p["sgn_b"], p["ws"], p["bs"], p["wout"], layer=l)
        x = _ffn_call(x, p["g2"], p["win"], p["cw"], p["cb"], p["wffn_out"], final_g,
                      layer=l, final_norm=(l == depth - 1))
    return x


def kernel(x_prompt, x_sample, norm1_g, w_in, hg_lb_logits, hg_norm_g, sg_norm_g, sg_norm_b,
           sg_w_spatial, sg_b_spatial, w_out, norm2_g, w_ffn_in, ffn_conv_w, ffn_conv_b,
           w_ffn_out, final_norm_g):
    depth = w_in.shape[0]
    hw = HG_WIDTH
    layers = []
    for l in range(depth):
        wl = w_in[l]
        q, f_fw, f_bw, i_in, rest = (wl[:, 0:hw], wl[:, hw:2 * hw], wl[:, 2 * hw:3 * hw],
                                     wl[:, 3 * hw:4 * hw], wl[:, 4 * hw:])
        gate_w = w_ffn_in[l][:, :D_FF].reshape(D_MODEL, FFN_STEPS, FFN_COLS)
        up_w = w_ffn_in[l][:, D_FF:].reshape(D_MODEL, FFN_STEPS, FFN_COLS)
        cw = ffn_conv_w[l]
        cb = ffn_conv_b[l]
        layers.append(dict(
            g1=norm1_g[l].reshape(1, D_MODEL),
            w_fwd=jnp.concatenate([f_fw, rest], axis=1).astype(BF16),
            w_bwd=jnp.concatenate([q, i_in, f_bw], axis=1).astype(BF16),
            logits_fwd=hg_lb_logits[0].astype(F32),
            logits_bwd=hg_lb_logits[1].astype(F32),
            hgn=hg_norm_g[l].reshape(1, hw),
            sgn_g=sg_norm_g[l].reshape(1, SG_WIDTH),
            sgn_b=sg_norm_b[l].reshape(1, SG_WIDTH),
            ws=sg_w_spatial[l],
            bs=sg_b_spatial[l].reshape(SG_HEADS, SG_CHUNK, 1),
            wout=w_out[l].astype(BF16),
            g2=norm2_g[l].reshape(1, D_MODEL),
            win=jnp.concatenate([gate_w, up_w], axis=2).transpose(1, 0, 2).astype(BF16),
            cw=jnp.concatenate([cw[:, :D_FF].reshape(3, FFN_STEPS, FFN_COLS),
                                cw[:, D_FF:].reshape(3, FFN_STEPS, FFN_COLS)],
                               axis=2).transpose(1, 0, 2),
            cb=jnp.concatenate([cb[:D_FF].reshape(FFN_STEPS, 1, FFN_COLS),
                                cb[D_FF:].reshape(FFN_STEPS, 1, FFN_COLS)], axis=2),
            wffn_out=w_ffn_out[l].reshape(FFN_STEPS, FFN_COLS, D_MODEL).astype(BF16),
        ))
    final_g = final_norm_g.reshape(1, D_MODEL)
    tables_fwd = _hgrn_tables(False)
    tables_bwd = _hgrn_tables(True)
    y_prompt = _trunk(x_prompt, layers, final_g, tables_fwd, tables_bwd)
    y_sample = _trunk(x_sample, layers, final_g, tables_fwd, tables_bwd)
    return (y_prompt, y_sample)
```

```python
import functools

import jax
import jax.numpy as jnp
import numpy as np
from jax import lax
from jax.experimental import pallas as pl
from jax.experimental.pallas import tpu as pltpu

D_MODEL = 1024
HG_HEADS = 4
HG_DIM = 128
HG_WIDTH = HG_HEADS * HG_DIM
HG_CHUNK = 64
SG_HEADS = 4
SG_DIM = 128
SG_WIDTH = SG_HEADS * SG_DIM
SG_CHUNK = 128
D_FF = 2816
EPS = 1e-6
LOG2E = 1.4426950408889634

W_Q, W_F_FW, W_F_BW, W_I, W_G, W_U, W_V = (0, HG_WIDTH, 2 * HG_WIDTH, 3 * HG_WIDTH, 4 * HG_WIDTH,
                                           5 * HG_WIDTH, 5 * HG_WIDTH + SG_WIDTH)

LANES = 128
SUBLANES = 8

LEVELS = (32, 16, 8, 4, 2)
SUMMED_LEVELS = tuple(m for m in LEVELS if m < 8)
N_LEVELS = len(LEVELS) + 1

FFN_COLS = 256
FFN_STEPS = D_FF // FFN_COLS
N_SLABS = 2 * FFN_COLS // LANES
HALO = 16

MIX_TILE = 512
OUT_ROWS = 128
FFN_TILE = 512
VMEM_LIMIT = 56 * 1024 * 1024

F32 = jnp.float32
BF16 = jnp.bfloat16

NT_DIMS = (((1,), (1,)), ((), ()))
TN_DIMS = (((0,), (0,)), ((), ()))


def _is_query_side(block_index, reverse):
    return ((block_index % 2) == 1) != reverse


def _hgrn_tables(reverse):
    c = HG_CHUNK
    r = np.arange(c)
    rr, tt = r[None, :], r[:, None]
    mats = [rr <= tt]
    masks = []
    for m in LEVELS + (1,):
        blk = r // m
        odd = (blk % 2) == 1
        ref = np.where(odd, blk * m - 1, (blk + 1) * m - 1)[:, None]
        if m in SUMMED_LEVELS:
            mats.append(np.where(odd[:, None], (rr > ref) & (rr <= tt), (rr > tt) & (rr <= ref)))
        masks.append(odd[:, None] & (blk[None, :] == blk[:, None] - 1))
    if reverse:
        mats = [a[::-1, ::-1] for a in mats]
        masks = [a[::-1, ::-1] for a in masks]
    sums = np.concatenate(mats, axis=0).astype(np.float32)
    sums2 = np.concatenate([sums, sums], axis=1)
    return (jnp.asarray(sums2, BF16), jnp.asarray(np.concatenate(masks, axis=0), F32))


def _rms(x, g):
    return x * lax.rsqrt(jnp.mean(x * x, axis=-1, keepdims=True) + EPS) * g


def _sigmoid(x):
    return 1.0 / (1.0 + jnp.exp(-x))


def _gelu_tanh(x):
    return 0.5 * x * (1.0 + jnp.tanh(0.7978845608028654 * (x + 0.044715 * (x * x * x))))


def _lower_bound(logits, layer):
    mx = jnp.max(logits, axis=0, keepdims=True)
    e = jnp.exp(logits - mx)
    p = e / jnp.sum(e, axis=0, keepdims=True)
    lb = jnp.zeros((1, HG_WIDTH), F32)
    for j in range(1, layer + 1):
        lb = lb + p[j:j + 1, :]
    return lb


def _hgrn_tile(qi_ref, f_ref, o_ref, st_ref, xl_ref, qs_ref, ks_ref, d_ref, sc_ref, lb,
               sums_ref, mask_ref, *, tile, reverse, f_off):
    c = HG_CHUNK
    n_chunks = tile // c
    last_row = 0 if reverse else c - 1

    def rows_of(ci):
        return pl.ds(pl.multiple_of(ci * c, c), c)

    def scale_pass(ci, carry):
        rows = rows_of(ci)
        q = qi_ref[rows, 0:HG_WIDTH]
        z = f_ref[rows, f_off:f_off + HG_WIDTH]
        v = qi_ref[rows, HG_WIDTH:2 * HG_WIDTH]
        sig = _sigmoid(z)
        f = lb + (1.0 - lb) * sig
        k = (1.0 - lb) * (1.0 - sig)
        lf2 = jnp.log(f) * LOG2E
        hi = lf2.astype(BF16)
        lo = (lf2 - hi.astype(F32)).astype(BF16)
        parts = jnp.concatenate([hi, lo], axis=0)
        e = jnp.dot(sums_ref[...], parts, preferred_element_type=F32)
        cum = e[0:c]
        tot = cum[last_row:last_row + 1]
        qs_ref[rows, :] = q * jnp.exp2(cum)
        ks_ref[rows, :] = k * jnp.exp2(tot - cum)
        d_ref[ci] = jnp.exp2(tot)
        row = lax.broadcasted_iota(jnp.int32, (c, HG_WIDTH), 0)
        for l, m in enumerate(LEVELS + (1,)):
            if m >= SUBLANES:
                blocks = []
                for r0 in range(0, c, m):
                    blk = slice(r0, r0 + m)
                    if _is_query_side(r0 // m, reverse):
                        ref = r0 + m if reverse else r0 - 1
                        blocks.append(q[blk] * jnp.exp2(cum[blk] - cum[ref:ref + 1]))
                    else:
                        ref = r0 if reverse else r0 + m - 1
                        blocks.append(k[blk] * jnp.exp2(cum[ref:ref + 1] - cum[blk]))
                qk = jnp.concatenate(blocks, axis=0)
            else:
                qk = jnp.where(_is_query_side(row // m, reverse), q * f if m == 1 else q, k)
                if m > 1:
                    i = 1 + SUMMED_LEVELS.index(m)
                    qk = qk * jnp.exp2(e[i * c:(i + 1) * c])
            xl_ref[l, rows, :] = qk.astype(BF16)
        for h in range(HG_HEADS):
            cols = slice(h * HG_DIM, (h + 1) * HG_DIM)
            diag = jnp.sum(q[:, cols] * k[:, cols], axis=-1, keepdims=True)
            o_ref[rows, cols] = diag * v[:, cols]
        return carry

    def score_pass(ci, carry):
        rows = rows_of(ci)
        for h in range(HG_HEADS):
            cols = slice(h * HG_DIM, (h + 1) * HG_DIM)
            scores = jnp.zeros((c, c), F32)
            for l in range(N_LEVELS):
                x = xl_ref[l, rows, cols]
                a = lax.dot_general(x, x, NT_DIMS, preferred_element_type=F32)
                scores = scores + a * mask_ref[l * c:(l + 1) * c, :]
            sc_ref[ci, h] = scores
        return carry

    def state_pass(i, carry):
        ci = (n_chunks - 1 - i) if reverse else i
        rows = rows_of(ci)
        d = d_ref[ci]
        for h in range(HG_HEADS):
            cols = slice(h * HG_DIM, (h + 1) * HG_DIM)
            v = qi_ref[rows, HG_WIDTH + h * HG_DIM:HG_WIDTH + (h + 1) * HG_DIM]
            st = st_ref[h]
            o = (lax.dot_general(qs_ref[rows, cols], st, NT_DIMS, preferred_element_type=F32)
                 + jnp.dot(sc_ref[ci, h], v, preferred_element_type=F32))
            o_ref[rows, cols] += o
            st_ref[h] = st * d[:, cols] + lax.dot_general(v, ks_ref[rows, cols], TN_DIMS,
                                                          preferred_element_type=F32)
        return carry

    lax.fori_loop(0, n_chunks, scale_pass, 0, unroll=True)
    lax.fori_loop(0, n_chunks, score_pass, 0, unroll=True)
    lax.fori_loop(0, n_chunks, state_pass, 0, unroll=True)


def _hgrn_scratch(tile):
    n_chunks = tile // HG_CHUNK
    return [
        pltpu.VMEM((HG_HEADS, HG_DIM, HG_DIM), F32),
        pltpu.VMEM((N_LEVELS, tile, HG_WIDTH), BF16),
        pltpu.VMEM((tile, HG_WIDTH), F32),
        pltpu.VMEM((tile, HG_WIDTH), F32),
        pltpu.VMEM((n_chunks, 1, HG_WIDTH), F32),
        pltpu.VMEM((n_chunks, HG_HEADS, HG_CHUNK, HG_CHUNK), F32),
    ]


def _bwd_kernel(x_ref, g1_ref, w_ref, logit_ref, sums_ref, mask_ref, o_ref, qi_ref,
                proj_ref, st_ref, xl_ref, qs_ref, ks_ref, d_ref, sc_ref, *, layer, tile):
    @pl.when(pl.program_id(1) == 0)
    def _():
        st_ref[...] = jnp.zeros_like(st_ref)

    for rows in (slice(0, tile // 2), slice(tile // 2, tile)):
        h = _rms(x_ref[0, rows, :], g1_ref[...]).astype(BF16)
        qi_ref[0, rows, 0:HG_WIDTH] = jnp.dot(h, w_ref[:, W_Q:W_Q + HG_WIDTH],
                                              preferred_element_type=F32)
        fi = jnp.dot(h, w_ref[:, W_F_BW:W_F_BW + 2 * HG_WIDTH], preferred_element_type=F32)
        proj_ref[rows, :] = fi[:, :HG_WIDTH]
        qi_ref[0, rows, HG_WIDTH:] = fi[:, HG_WIDTH:]
    lb = _lower_bound(logit_ref[...], layer)
    _hgrn_tile(qi_ref.at[0], proj_ref, o_ref.at[0], st_ref, xl_ref, qs_ref, ks_ref, d_ref,
               sc_ref, lb, sums_ref, mask_ref, tile=tile, reverse=True, f_off=0)


def _mix_kernel(x_ref, obw_ref, qi_ref, g1_ref, w_ref, logit_ref, sums_ref, mask_ref,
                hgn_ref, sgn_g_ref, sgn_b_ref, ws_ref, bs_ref, wout_ref, o_ref,
                proj_ref, ofw_ref, sg_ref, acc_ref, st_ref, xl_ref, qs_ref, ks_ref, d_ref, sc_ref,
                *, layer, tile):
    @pl.when(pl.program_id(1) == 0)
    def _():
        st_ref[...] = jnp.zeros_like(st_ref)

    x = x_ref[0]
    h = _rms(x, g1_ref[...]).astype(BF16)
    g_off, u_off, v_off = HG_WIDTH, 2 * HG_WIDTH, 2 * HG_WIDTH + SG_WIDTH
    for rows in (slice(0, tile // 2), slice(tile // 2, tile)):
        proj_ref[rows, u_off:] = jnp.dot(h[rows], w_ref[:, W_U:W_U + 2 * SG_WIDTH],
                                         preferred_element_type=F32)
    proj_ref[:, 0:g_off] = jnp.dot(h, w_ref[:, W_F_FW:W_F_FW + HG_WIDTH],
                                   preferred_element_type=F32)
    proj_ref[:, g_off:u_off] = jnp.dot(h, w_ref[:, W_G:W_G + HG_WIDTH],
                                       preferred_element_type=F32)

    for hd in range(SG_HEADS):
        cols = slice(hd * SG_DIM, (hd + 1) * SG_DIM)
        bias = jnp.broadcast_to(bs_ref[hd], (SG_CHUNK, SG_DIM))
        for pc in range(tile // SG_CHUNK):
            rows = slice(pc * SG_CHUNK, (pc + 1) * SG_CHUNK)
            vv = _gelu_tanh(proj_ref[rows, v_off + hd * SG_DIM:v_off + (hd + 1) * SG_DIM])
            mu = jnp.mean(vv, axis=-1, keepdims=True)
            vc = vv - mu
            vn = vc * lax.rsqrt(jnp.mean(vc * vc, axis=-1, keepdims=True) + EPS)
            vn = vn * sgn_g_ref[:, cols] + sgn_b_ref[:, cols]
            sp = jnp.dot(ws_ref[hd], vn, preferred_element_type=F32) + bias
            u = _gelu_tanh(proj_ref[rows, u_off + hd * SG_DIM:u_off + (hd + 1) * SG_DIM])
            sg_ref[rows, cols] = (u * sp).astype(BF16)
    acc_ref[...] = x + jnp.dot(sg_ref[...], wout_ref[HG_WIDTH:, :], preferred_element_type=F32)

    lb = _lower_bound(logit_ref[...], layer)
    _hgrn_tile(qi_ref.at[0], proj_ref, ofw_ref, st_ref, xl_ref, qs_ref, ks_ref, d_ref, sc_ref,
               lb, sums_ref, mask_ref, tile=tile, reverse=False, f_off=0)

    for rb in range(tile // OUT_ROWS):
        rows = slice(rb * OUT_ROWS, (rb + 1) * OUT_ROWS)
        heads = []
        for hd in range(HG_HEADS):
            cols = slice(hd * HG_DIM, (hd + 1) * HG_DIM)
            o = _rms(ofw_ref[rows, cols] + obw_ref[0, rows, cols], hgn_ref[:, cols])
            gate = proj_ref[rows, g_off + hd * HG_DIM:g_off + (hd + 1) * HG_DIM]
            heads.append((o * (gate * _sigmoid(gate))).astype(BF16))
        o_ref[0, rows, :] = acc_ref[rows, :] + jnp.dot(
            jnp.concatenate(heads, axis=1), wout_ref[:HG_WIDTH, :], preferred_element_type=F32)


def _ffn_kernel(x_ref, xp_ref, xn_ref, g2_ref, win_ref, cw_ref, cb_ref, wout_ref, gf_ref,
                o_ref, h_ref, a_ref, act_ref, *, tile, final_norm):
    j = pl.program_id(1)
    n = pl.num_programs(1)
    x = x_ref[0]
    g2 = g2_ref[...]
    h_ref[0:tile, :] = _rms(x, g2).astype(BF16)
    hp = _rms(xp_ref[0, HALO - 1:HALO, :], g2) * jnp.where(j > 0, 1.0, 0.0)
    hn = _rms(xn_ref[0, 0:1, :], g2) * jnp.where(j < n - 1, 1.0, 0.0)
    rid = lax.broadcasted_iota(jnp.int32, (HALO, D_MODEL), 0)
    h_ref[tile:tile + HALO, :] = jnp.where(
        rid == 0, hp, jnp.where(rid == 1, hn, 0.0)).astype(BF16)

    def slab_cols(s, sl):
        half, part = divmod(sl, N_SLABS // 2)
        start = half * D_FF + s * FFN_COLS + part * LANES
        return slice(start, start + LANES)

    def up_proj(s):
        bounds = (0, tile // 2, tile + HALO) if s == 0 else (0, tile + HALO)
        for lo, hi in zip(bounds[:-1], bounds[1:]):
            n_main = min(hi, tile) - lo
            for half in range(2):
                c0 = half * D_FF + s * FFN_COLS
                a = jnp.dot(h_ref[lo:hi, :], win_ref[:, c0:c0 + FFN_COLS],
                            preferred_element_type=F32)
                for part in range(N_SLABS // 2):
                    sl = half * (N_SLABS // 2) + part
                    cols = slice(part * LANES, (part + 1) * LANES)
                    a_ref[s, sl, HALO + lo:HALO + lo + n_main, :] = a[0:n_main, cols]
                    if hi > tile:
                        a_ref[s, sl, HALO - 1:HALO, :] = a[n_main:n_main + 1, cols]
                        a_ref[s, sl, HALO + tile:HALO + tile + 1, :] = a[n_main + 1:n_main + 2,
                                                                         cols]

    def conv_gate(s):
        ys = []
        for sl in range(N_SLABS):
            cols = slab_cols(s, sl)
            ys.append(a_ref[s, sl, pl.ds(HALO - 1, tile), :] * cw_ref[0:1, cols]
                      + a_ref[s, sl, pl.ds(HALO, tile), :] * cw_ref[1:2, cols]
                      + a_ref[s, sl, pl.ds(HALO + 1, tile), :] * cw_ref[2:3, cols]
                      + cb_ref[:, cols])
        half = N_SLABS // 2
        gate = jnp.concatenate(ys[:half], axis=1)
        up = jnp.concatenate(ys[half:], axis=1)
        act_ref[s] = (gate * _sigmoid(gate) * up).astype(BF16)

    up_proj(0)
    for s in range(FFN_STEPS):
        if s + 1 < FFN_STEPS:
            up_proj(s + 1)
        conv_gate(s)

    acc = x
    for s in range(FFN_STEPS):
        acc = acc + jnp.dot(act_ref[s], wout_ref[s], preferred_element_type=F32)
    if final_norm:
        acc = _rms(acc, gf_ref[...])
    o_ref[0] = acc


def _const_spec(shape):
    nd = len(shape)
    return pl.BlockSpec(shape, lambda b, j: (0,) * nd, pipeline_mode=pl.Buffered(1))


def _params():
    return pltpu.CompilerParams(dimension_semantics=("arbitrary", "arbitrary"),
                                vmem_limit_bytes=VMEM_LIMIT)


def _bwd_call(x, g1, w_in, logits, tables, *, layer):
    bsz, seq, _ = x.shape
    tile = MIX_TILE
    nt = seq // tile
    consts = (g1, w_in, logits) + tuple(tables)

    def rev(b, j):
        return (b, nt - 1 - j, 0)

    return pl.pallas_call(
        functools.partial(_bwd_kernel, layer=layer, tile=tile),
        out_shape=(jax.ShapeDtypeStruct((bsz, seq, HG_WIDTH), F32),
                   jax.ShapeDtypeStruct((bsz, seq, 2 * HG_WIDTH), F32)),
        grid=(bsz, nt),
        in_specs=[pl.BlockSpec((1, tile, D_MODEL), rev)]
        + [_const_spec(c.shape) for c in consts],
        out_specs=(pl.BlockSpec((1, tile, HG_WIDTH), rev),
                   pl.BlockSpec((1, tile, 2 * HG_WIDTH), rev)),
        scratch_shapes=[pltpu.VMEM((tile, HG_WIDTH), F32)] + _hgrn_scratch(tile),
        compiler_params=_params(),
        name=f"hgrn_bwd_l{layer}",
    )(x, *consts)


def _mix_call(x, obw, qi, g1, w_in, logits, tables, hgn, sgn_g, sgn_b, ws, bs, wout, *, layer):
    bsz, seq, _ = x.shape
    tile = MIX_TILE
    nt = seq // tile
    consts = (g1, w_in, logits) + tuple(tables) + (hgn, sgn_g, sgn_b, ws, bs, wout)
    return pl.pallas_call(
        functools.partial(_mix_kernel, layer=layer, tile=tile),
        out_shape=jax.ShapeDtypeStruct((bsz, seq, D_MODEL), F32),
        grid=(bsz, nt),
        in_specs=[
            pl.BlockSpec((1, tile, D_MODEL), lambda b, j: (b, j, 0)),
            pl.BlockSpec((1, tile, HG_WIDTH), lambda b, j: (b, j, 0)),
            pl.BlockSpec((1, tile, 2 * HG_WIDTH), lambda b, j: (b, j, 0)),
        ] + [_const_spec(c.shape) for c in consts],
        out_specs=pl.BlockSpec((1, tile, D_MODEL), lambda b, j: (b, j, 0)),
        scratch_shapes=[
            pltpu.VMEM((tile, 2 * HG_WIDTH + 2 * SG_WIDTH), F32),
            pltpu.VMEM((tile, HG_WIDTH), F32),
            pltpu.VMEM((tile, SG_WIDTH), BF16),
            pltpu.VMEM((tile, D_MODEL), F32),
        ] + _hgrn_scratch(tile),
        compiler_params=_params(),
        name=f"mixer_l{layer}",
    )(x, obw, qi, *consts)


def _ffn_call(x, g2, win, cw, cb, wout, gf, *, layer, final_norm):
    bsz, seq, _ = x.shape
    tile = FFN_TILE
    nt = seq // tile
    per = tile // HALO
    last = seq // HALO - 1
    consts = (g2, win, cw, cb, wout, gf)
    return pl.pallas_call(
        functools.partial(_ffn_kernel, tile=tile, final_norm=final_norm),
        out_shape=jax.ShapeDtypeStruct((bsz, seq, D_MODEL), F32),
        grid=(bsz, nt),
        in_specs=[
            pl.BlockSpec((1, tile, D_MODEL), lambda b, j: (b, j, 0)),
            pl.BlockSpec((1, HALO, D_MODEL), lambda b, j: (b, jnp.maximum(j * per - 1, 0), 0)),
            pl.BlockSpec((1, HALO, D_MODEL),
                         lambda b, j: (b, jnp.minimum((j + 1) * per, last), 0)),
        ] + [_const_spec(c.shape) for c in consts],
        out_specs=pl.BlockSpec((1, tile, D_MODEL), lambda b, j: (b, j, 0)),
        scratch_shapes=[
            pltpu.VMEM((tile + HALO, D_MODEL), BF16),
            pltpu.VMEM((FFN_STEPS, N_SLABS, tile + 2 * HALO, LANES), F32),
            pltpu.VMEM((FFN_STEPS, tile, FFN_COLS), BF16),
        ],
        compiler_params=_params(),
        name=f"ffn_l{layer}",
    )(x, x, x, *consts)


def _trunk(x, layers, final_g, tables_fwd, tables_bwd):
    depth = len(layers)
    for l, p in enumerate(layers):
        obw, qi = _bwd_call(x, p["g1"], p["w_in"], p["logits_bwd"], tables_bwd, layer=l)
        x = _mix_call(x, obw, qi, p["g1"], p["w_in"], p["logits_fwd"], tables_fwd, p["hgn"],
                      p["sgn_g"], p["sgn_b"], p["ws"], p["bs"], p["wout"], layer=l)
        x = _ffn_call(x, p["g2"], p["win"], p["cw"], p["cb"], p["wffn_out"], final_g,
                      layer=l, final_norm=(l == depth - 1))
    return x


def kernel(x_prompt, x_sample, norm1_g, w_in, hg_lb_logits, hg_norm_g, sg_norm_g, sg_norm_b,
           sg_w_spatial, sg_b_spatial, w_out, norm2_g, w_ffn_in, ffn_conv_w, ffn_conv_b,
           w_ffn_out, final_norm_g):
    depth = w_in.shape[0]
    w_in_b, w_out_b = w_in.astype(BF16), w_out.astype(BF16)
    w_ffn_in_b, w_ffn_out_b = w_ffn_in.astype(BF16), w_ffn_out.astype(BF16)
    layers = []
    for l in range(depth):
        layers.append(dict(
            g1=norm1_g[l].reshape(1, D_MODEL),
            w_in=w_in_b[l],
            logits_fwd=hg_lb_logits[0].astype(F32),
            logits_bwd=hg_lb_logits[1].astype(F32),
            hgn=hg_norm_g[l].reshape(1, HG_WIDTH),
            sgn_g=sg_norm_g[l].reshape(1, SG_WIDTH),
            sgn_b=sg_norm_b[l].reshape(1, SG_WIDTH),
            ws=sg_w_spatial[l],
            bs=sg_b_spatial[l].reshape(SG_HEADS, SG_CHUNK, 1),
            wout=w_out_b[l],
            g2=norm2_g[l].reshape(1, D_MODEL),
            win=w_ffn_in_b[l],
            cw=ffn_conv_w[l],
            cb=ffn_conv_b[l].reshape(1, 2 * D_FF),
            wffn_out=w_ffn_out_b[l].reshape(FFN_STEPS, FFN_COLS, D_MODEL),
        ))
    final_g = final_norm_g.reshape(1, D_MODEL)
    tables_fwd = _hgrn_tables(False)
    tables_bwd = _hgrn_tables(True)
    y_prompt = _trunk(x_prompt, layers, final_g, tables_fwd, tables_bwd)
    y_sample = _trunk(x_sample, layers, final_g, tables_fwd, tables_bwd)
    return (y_prompt, y_sample)
```

```python
import functools

import jax
import jax.numpy as jnp
import numpy as np
from jax import lax
from jax.experimental import pallas as pl
from jax.experimental.pallas import tpu as pltpu

D_MODEL = 1024
HG_HEADS = 4
HG_DIM = 128
HG_WIDTH = HG_HEADS * HG_DIM
HG_CHUNK = 64
SG_HEADS = 4
SG_DIM = 128
SG_WIDTH = SG_HEADS * SG_DIM
SG_CHUNK = 128
D_FF = 2816
EPS = 1e-6
LOG2E = 1.4426950408889634

W_Q, W_F_FW, W_F_BW, W_I, W_G, W_U, W_V = (0, HG_WIDTH, 2 * HG_WIDTH, 3 * HG_WIDTH, 4 * HG_WIDTH,
                                           5 * HG_WIDTH, 5 * HG_WIDTH + SG_WIDTH)

LANES = 128
SUBLANES = 8

LEVELS = (32, 16, 8, 4, 2)
SUMMED_LEVELS = tuple(m for m in LEVELS if m < 8)
N_LEVELS = len(LEVELS) + 1

FFN_COLS = 256
FFN_STEPS = D_FF // FFN_COLS
N_SLABS = 2 * FFN_COLS // LANES
HALO = 16

MIX_TILE = 512
OUT_ROWS = 128
FFN_TILE = 512
VMEM_LIMIT = 56 * 1024 * 1024

F32 = jnp.float32
BF16 = jnp.bfloat16

NT_DIMS = (((1,), (1,)), ((), ()))
TN_DIMS = (((0,), (0,)), ((), ()))


def _is_query_side(block_index, reverse):
    return ((block_index % 2) == 1) != reverse


def _hgrn_tables(reverse):
    c = HG_CHUNK
    r = np.arange(c)
    rr, tt = r[None, :], r[:, None]
    mats = [rr <= tt]
    masks = []
    for m in LEVELS + (1,):
        blk = r // m
        odd = (blk % 2) == 1
        ref = np.where(odd, blk * m - 1, (blk + 1) * m - 1)[:, None]
        if m in SUMMED_LEVELS:
            mats.append(np.where(odd[:, None], (rr > ref) & (rr <= tt), (rr > tt) & (rr <= ref)))
        masks.append(odd[:, None] & (blk[None, :] == blk[:, None] - 1))
    if reverse:
        mats = [a[::-1, ::-1] for a in mats]
        masks = [a[::-1, ::-1] for a in masks]
    sums = np.concatenate(mats, axis=0).astype(np.float32)
    sums2 = np.concatenate([sums, sums], axis=1)
    return (jnp.asarray(sums2, BF16), jnp.asarray(np.concatenate(masks, axis=0), F32))


def _rms(x, g):
    return x * lax.rsqrt(jnp.mean(x * x, axis=-1, keepdims=True) + EPS) * g


def _sigmoid(x):
    return 1.0 / (1.0 + jnp.exp(-x))


def _gelu_tanh(x):
    return 0.5 * x * (1.0 + jnp.tanh(0.7978845608028654 * (x + 0.044715 * (x * x * x))))


def _lower_bound(logits, layer):
    mx = jnp.max(logits, axis=0, keepdims=True)
    e = jnp.exp(logits - mx)
    p = e / jnp.sum(e, axis=0, keepdims=True)
    lb = jnp.zeros((1, HG_WIDTH), F32)
    for j in range(1, layer + 1):
        lb = lb + p[j:j + 1, :]
    return lb


def _hgrn_tile(qi_ref, f_ref, o_ref, st_ref, xl_ref, qs_ref, ks_ref, d_ref, sc_ref, lb,
               sums_ref, mask_ref, *, tile, reverse, f_off):
    c = HG_CHUNK
    n_chunks = tile // c
    last_row = 0 if reverse else c - 1

    def rows_of(ci):
        return pl.ds(pl.multiple_of(ci * c, c), c)

    def scale_pass(ci, carry):
        rows = rows_of(ci)
        q = qi_ref[rows, 0:HG_WIDTH]
        z = f_ref[rows, f_off:f_off + HG_WIDTH]
        v = qi_ref[rows, HG_WIDTH:2 * HG_WIDTH]
        sig = _sigmoid(z)
        f = lb + (1.0 - lb) * sig
        k = (1.0 - lb) * (1.0 - sig)
        lf2 = jnp.log(f) * LOG2E
        hi = lf2.astype(BF16)
        lo = (lf2 - hi.astype(F32)).astype(BF16)
        parts = jnp.concatenate([hi, lo], axis=0)
        e = jnp.dot(sums_ref[...], parts, preferred_element_type=F32)
        cum = e[0:c]
        tot = cum[last_row:last_row + 1]
        qs_ref[rows, :] = q * jnp.exp2(cum)
        ks_ref[rows, :] = k * jnp.exp2(tot - cum)
        d_ref[ci] = jnp.exp2(tot)
        row = lax.broadcasted_iota(jnp.int32, (c, HG_WIDTH), 0)
        for l, m in enumerate(LEVELS + (1,)):
            if m >= SUBLANES:
                blocks = []
                for r0 in range(0, c, m):
                    blk = slice(r0, r0 + m)
                    if _is_query_side(r0 // m, reverse):
                        ref = r0 + m if reverse else r0 - 1
                        blocks.append(q[blk] * jnp.exp2(cum[blk] - cum[ref:ref + 1]))
                    else:
                        ref = r0 if reverse else r0 + m - 1
                        blocks.append(k[blk] * jnp.exp2(cum[ref:ref + 1] - cum[blk]))
                qk = jnp.concatenate(blocks, axis=0)
            else:
                qk = jnp.where(_is_query_side(row // m, reverse), q * f if m == 1 else q, k)
                if m > 1:
                    i = 1 + SUMMED_LEVELS.index(m)
                    qk = qk * jnp.exp2(e[i * c:(i + 1) * c])
            xl_ref[l, rows, :] = qk.astype(BF16)
        for h in range(HG_HEADS):
            cols = slice(h * HG_DIM, (h + 1) * HG_DIM)
            diag = jnp.sum(q[:, cols] * k[:, cols], axis=-1, keepdims=True)
            o_ref[rows, cols] = diag * v[:, cols]
        return carry

    def score_pass(ci, carry):
        rows = rows_of(ci)
        for h in range(HG_HEADS):
            cols = slice(h * HG_DIM, (h + 1) * HG_DIM)
            scores = jnp.zeros((c, c), F32)
            for l in range(N_LEVELS):
                x = xl_ref[l, rows, cols]
                a = lax.dot_general(x, x, NT_DIMS, preferred_element_type=F32)
                scores = scores + a * mask_ref[l * c:(l + 1) * c, :]
            sc_ref[ci, h] = scores
        return carry

    def state_pass(i, carry):
        ci = (n_chunks - 1 - i) if reverse else i
        rows = rows_of(ci)
        d = d_ref[ci]
        for h in range(HG_HEADS):
            cols = slice(h * HG_DIM, (h + 1) * HG_DIM)
            v = qi_ref[rows, HG_WIDTH + h * HG_DIM:HG_WIDTH + (h + 1) * HG_DIM]
            st = st_ref[h]
            o = (lax.dot_general(qs_ref[rows, cols], st, NT_DIMS, preferred_element_type=F32)
                 + jnp.dot(sc_ref[ci, h], v, preferred_element_type=F32))
            o_ref[rows, cols] += o
            st_ref[h] = st * d[:, cols] + lax.dot_general(v, ks_ref[rows, cols], TN_DIMS,
                                                          preferred_element_type=F32)
        return carry

    lax.fori_loop(0, n_chunks, scale_pass, 0, unroll=True)
    lax.fori_loop(0, n_chunks, score_pass, 0, unroll=True)
    lax.fori_loop(0, n_chunks, state_pass, 0, unroll=True)


def _hgrn_scratch(tile):
    n_chunks = tile // HG_CHUNK
    return [
        pltpu.VMEM((HG_HEADS, HG_DIM, HG_DIM), F32),
        pltpu.VMEM((N_LEVELS, tile, HG_WIDTH), BF16),
        pltpu.VMEM((tile, HG_WIDTH), F32),
        pltpu.VMEM((tile, HG_WIDTH), F32),
        pltpu.VMEM((n_chunks, 1, HG_WIDTH), F32),
        pltpu.VMEM((n_chunks, HG_HEADS, HG_CHUNK, HG_CHUNK), F32),
    ]


def _bwd_kernel(x_ref, g1_ref, w_ref, logit_ref, sums_ref, mask_ref, o_ref, qi_ref,
                proj_ref, st_ref, xl_ref, qs_ref, ks_ref, d_ref, sc_ref, *, layer, tile):
    @pl.when(pl.program_id(1) == 0)
    def _():
        st_ref[...] = jnp.zeros_like(st_ref)

    for rows in (slice(0, tile // 2), slice(tile // 2, tile)):
        h = _rms(x_ref[0, rows, :], g1_ref[...]).astype(BF16)
        qi_ref[0, rows, 0:HG_WIDTH] = jnp.dot(h, w_ref[:, W_Q:W_Q + HG_WIDTH],
                                              preferred_element_type=F32)
        fi = jnp.dot(h, w_ref[:, W_F_BW:W_F_BW + 2 * HG_WIDTH], preferred_element_type=F32)
        proj_ref[rows, :] = fi[:, :HG_WIDTH]
        qi_ref[0, rows, HG_WIDTH:] = fi[:, HG_WIDTH:]
    lb = _lower_bound(logit_ref[...], layer)
    _hgrn_tile(qi_ref.at[0], proj_ref, o_ref.at[0], st_ref, xl_ref, qs_ref, ks_ref, d_ref,
               sc_ref, lb, sums_ref, mask_ref, tile=tile, reverse=True, f_off=0)


def _mix_kernel(x_ref, obw_ref, qi_ref, g1_ref, w_ref, logit_ref, sums_ref, mask_ref,
                hgn_ref, sgn_g_ref, sgn_b_ref, ws_ref, bs_ref, wout_ref, o_ref,
                proj_ref, ofw_ref, sg_ref, acc_ref, st_ref, xl_ref, qs_ref, ks_ref, d_ref, sc_ref,
                *, layer, tile):
    @pl.when(pl.program_id(1) == 0)
    def _():
        st_ref[...] = jnp.zeros_like(st_ref)

    x = x_ref[0]
    h = _rms(x, g1_ref[...]).astype(BF16)
    g_off, u_off, v_off = HG_WIDTH, 2 * HG_WIDTH, 2 * HG_WIDTH + SG_WIDTH
    for rows in (slice(0, tile // 2), slice(tile // 2, tile)):
        proj_ref[rows, u_off:] = jnp.dot(h[rows], w_ref[:, W_U:W_U + 2 * SG_WIDTH],
                                         preferred_element_type=F32)
    proj_ref[:, 0:g_off] = jnp.dot(h, w_ref[:, W_F_FW:W_F_FW + HG_WIDTH],
                                   preferred_element_type=F32)
    proj_ref[:, g_off:u_off] = jnp.dot(h, w_ref[:, W_G:W_G + HG_WIDTH],
                                       preferred_element_type=F32)

    for hd in range(SG_HEADS):
        cols = slice(hd * SG_DIM, (hd + 1) * SG_DIM)
        bias = jnp.broadcast_to(bs_ref[hd], (SG_CHUNK, SG_DIM))
        for pc in range(tile // SG_CHUNK):
            rows = slice(pc * SG_CHUNK, (pc + 1) * SG_CHUNK)
            vv = _gelu_tanh(proj_ref[rows, v_off + hd * SG_DIM:v_off + (hd + 1) * SG_DIM])
            mu = jnp.mean(vv, axis=-1, keepdims=True)
            vc = vv - mu
            vn = vc * lax.rsqrt(jnp.mean(vc * vc, axis=-1, keepdims=True) + EPS)
            vn = vn * sgn_g_ref[:, cols] + sgn_b_ref[:, cols]
            sp = jnp.dot(ws_ref[hd], vn, preferred_element_type=F32) + bias
            u = _gelu_tanh(proj_ref[rows, u_off + hd * SG_DIM:u_off + (hd + 1) * SG_DIM])
            sg_ref[rows, cols] = (u * sp).astype(BF16)
    acc_ref[...] = x + jnp.dot(sg_ref[...], wout_ref[HG_WIDTH:, :], preferred_element_type=F32)

    lb = _lower_bound(logit_ref[...], layer)
    _hgrn_tile(qi_ref.at[0], proj_ref, ofw_ref, st_ref, xl_ref, qs_ref, ks_ref, d_ref, sc_ref,
               lb, sums_ref, mask_ref, tile=tile, reverse=False, f_off=0)

    for rb in range(tile // OUT_ROWS):
        rows = slice(rb * OUT_ROWS, (rb + 1) * OUT_ROWS)
        heads = []
        for hd in range(HG_HEADS):
            cols = slice(hd * HG_DIM, (hd + 1) * HG_DIM)
            o = _rms(ofw_ref[rows, cols] + obw_ref[0, rows, cols], hgn_ref[:, cols])
            gate = proj_ref[rows, g_off + hd * HG_DIM:g_off + (hd + 1) * HG_DIM]
            heads.append((o * (gate * _sigmoid(gate))).astype(BF16))
        o_ref[0, rows, :] = acc_ref[rows, :] + jnp.dot(
            jnp.concatenate(heads, axis=1), wout_ref[:HG_WIDTH, :], preferred_element_type=F32)


def _ffn_kernel(x_ref, xp_ref, xn_ref, g2_ref, win_ref, cw_ref, cb_ref, wout_ref, gf_ref,
                o_ref, h_ref, a_ref, act_ref, *, tile, final_norm):
    j = pl.program_id(1)
    n = pl.num_programs(1)
    x = x_ref[0]
    g2 = g2_ref[...]
    h_ref[0:tile, :] = _rms(x, g2).astype(BF16)
    hp = _rms(xp_ref[0, HALO - 1:HALO, :], g2) * jnp.where(j > 0, 1.0, 0.0)
    hn = _rms(xn_ref[0, 0:1, :], g2) * jnp.where(j < n - 1, 1.0, 0.0)
    rid = lax.broadcasted_iota(jnp.int32, (HALO, D_MODEL), 0)
    h_ref[tile:tile + HALO, :] = jnp.where(
        rid == 0, hp, jnp.where(rid == 1, hn, 0.0)).astype(BF16)

    def slab_cols(s, sl):
        half, part = divmod(sl, N_SLABS // 2)
        start = half * D_FF + s * FFN_COLS + part * LANES
        return slice(start, start + LANES)

    def up_proj(s):
        bounds = (0, tile // 2, tile + HALO) if s == 0 else (0, tile + HALO)
        for lo, hi in zip(bounds[:-1], bounds[1:]):
            n_main = min(hi, tile) - lo
            for half in range(2):
                c0 = half * D_FF + s * FFN_COLS
                a = jnp.dot(h_ref[lo:hi, :], win_ref[:, c0:c0 + FFN_COLS],
                            preferred_element_type=F32)
                for part in range(N_SLABS // 2):
                    sl = half * (N_SLABS // 2) + part
                    cols = slice(part * LANES, (part + 1) * LANES)
                    a_ref[s, sl, HALO + lo:HALO + lo + n_main, :] = a[0:n_main, cols]
                    if hi > tile:
                        a_ref[s, sl, HALO - 1:HALO, :] = a[n_main:n_main + 1, cols]
                        a_ref[s, sl, HALO + tile:HALO + tile + 1, :] = a[n_main + 1:n_main + 2,
                                                                         cols]

    def conv_gate(s):
        ys = []
        for sl in range(N_SLABS):
            cols = slab_cols(s, sl)
            ys.append(a_ref[s, sl, pl.ds(HALO - 1, tile), :] * cw_ref[0:1, cols]
                      + a_ref[s, sl, pl.ds(HALO, tile), :] * cw_ref[1:2, cols]
                      + a_ref[s, sl, pl.ds(HALO + 1, tile), :] * cw_ref[2:3, cols]
                      + cb_ref[:, cols])
        half = N_SLABS // 2
        gate = jnp.concatenate(ys[:half], axis=1)
        up = jnp.concatenate(ys[half:], axis=1)
        act_ref[s] = (gate * _sigmoid(gate) * up).astype(BF16)

    up_proj(0)
    for s in range(FFN_STEPS):
        if s + 1 < FFN_STEPS:
            up_proj(s + 1)
        conv_gate(s)

    acc = x
    for s in range(FFN_STEPS):
        acc = acc + jnp.dot(act_ref[s], wout_ref[s], preferred_element_type=F32)
    if final_norm:
        acc = _rms(acc, gf_ref[...])
    o_ref[0] = acc


class _PerLayer:
    def __init__(self, array):
        self.array = array


def _operand(c):
    return c.array if isinstance(c, _PerLayer) else c


def _const_spec(c, layer):
    if isinstance(c, _PerLayer):
        nd = c.array.ndim - 1
        return pl.BlockSpec((None,) + c.array.shape[1:], lambda b, j: (layer,) + (0,) * nd,
                            pipeline_mode=pl.Buffered(1))
    nd = c.ndim
    return pl.BlockSpec(c.shape, lambda b, j: (0,) * nd, pipeline_mode=pl.Buffered(1))


def _params():
    return pltpu.CompilerParams(dimension_semantics=("arbitrary", "arbitrary"),
                                vmem_limit_bytes=VMEM_LIMIT)


def _bwd_call(x, g1, w_in, logits, tables, *, layer):
    bsz, seq, _ = x.shape
    tile = MIX_TILE
    nt = seq // tile
    consts = (g1, w_in, logits) + tuple(tables)

    def rev(b, j):
        return (b, nt - 1 - j, 0)

    return pl.pallas_call(
        functools.partial(_bwd_kernel, layer=layer, tile=tile),
        out_shape=(jax.ShapeDtypeStruct((bsz, seq, HG_WIDTH), F32),
                   jax.ShapeDtypeStruct((bsz, seq, 2 * HG_WIDTH), F32)),
        grid=(bsz, nt),
        in_specs=[pl.BlockSpec((1, tile, D_MODEL), rev)]
        + [_const_spec(c, layer) for c in consts],
        out_specs=(pl.BlockSpec((1, tile, HG_WIDTH), rev),
                   pl.BlockSpec((1, tile, 2 * HG_WIDTH), rev)),
        scratch_shapes=[pltpu.VMEM((tile, HG_WIDTH), F32)] + _hgrn_scratch(tile),
        compiler_params=_params(),
        name=f"hgrn_bwd_l{layer}",
    )(x, *map(_operand, consts))


def _mix_call(x, obw, qi, g1, w_in, logits, tables, hgn, sgn_g, sgn_b, ws, bs, wout, *, layer):
    bsz, seq, _ = x.shape
    tile = MIX_TILE
    nt = seq // tile
    consts = (g1, w_in, logits) + tuple(tables) + (hgn, sgn_g, sgn_b, ws, bs, wout)
    return pl.pallas_call(
        functools.partial(_mix_kernel, layer=layer, tile=tile),
        out_shape=jax.ShapeDtypeStruct((bsz, seq, D_MODEL), F32),
        grid=(bsz, nt),
        in_specs=[
            pl.BlockSpec((1, tile, D_MODEL), lambda b, j: (b, j, 0)),
            pl.BlockSpec((1, tile, HG_WIDTH), lambda b, j: (b, j, 0)),
            pl.BlockSpec((1, tile, 2 * HG_WIDTH), lambda b, j: (b, j, 0)),
        ] + [_const_spec(c, layer) for c in consts],
        out_specs=pl.BlockSpec((1, tile, D_MODEL), lambda b, j: (b, j, 0)),
        scratch_shapes=[
            pltpu.VMEM((tile, 2 * HG_WIDTH + 2 * SG_WIDTH), F32),
            pltpu.VMEM((tile, HG_WIDTH), F32),
            pltpu.VMEM((tile, SG_WIDTH), BF16),
            pltpu.VMEM((tile, D_MODEL), F32),
        ] + _hgrn_scratch(tile),
        compiler_params=_params(),
        name=f"mixer_l{layer}",
    )(x, obw, qi, *map(_operand, consts))


def _ffn_call(x, g2, win, cw, cb, wout, gf, *, layer, final_norm):
    bsz, seq, _ = x.shape
    tile = FFN_TILE
    nt = seq // tile
    per = tile // HALO
    last = seq // HALO - 1
    consts = (g2, win, cw, cb, wout, gf)
    return pl.pallas_call(
        functools.partial(_ffn_kernel, tile=tile, final_norm=final_norm),
        out_shape=jax.ShapeDtypeStruct((bsz, seq, D_MODEL), F32),
        grid=(bsz, nt),
        in_specs=[
            pl.BlockSpec((1, tile, D_MODEL), lambda b, j: (b, j, 0)),
            pl.BlockSpec((1, HALO, D_MODEL), lambda b, j: (b, jnp.maximum(j * per - 1, 0), 0)),
            pl.BlockSpec((1, HALO, D_MODEL),
                         lambda b, j: (b, jnp.minimum((j + 1) * per, last), 0)),
        ] + [_const_spec(c, layer) for c in consts],
        out_specs=pl.BlockSpec((1, tile, D_MODEL), lambda b, j: (b, j, 0)),
        scratch_shapes=[
            pltpu.VMEM((tile + HALO, D_MODEL), BF16),
            pltpu.VMEM((FFN_STEPS, N_SLABS, tile + 2 * HALO, LANES), F32),
            pltpu.VMEM((FFN_STEPS, tile, FFN_COLS), BF16),
        ],
        compiler_params=_params(),
        name=f"ffn_l{layer}",
    )(x, x, x, *map(_operand, consts))


def _trunk(x, layers, final_g, tables_fwd, tables_bwd):
    depth = len(layers)
    for l, p in enumerate(layers):
        obw, qi = _bwd_call(x, p["g1"], p["w_in"], p["logits_bwd"], tables_bwd, layer=l)
        x = _mix_call(x, obw, qi, p["g1"], p["w_in"], p["logits_fwd"], tables_fwd, p["hgn"],
                      p["sgn_g"], p["sgn_b"], p["ws"], p["bs"], p["wout"], layer=l)
        x = _ffn_call(x, p["g2"], p["win"], p["cw"], p["cb"], p["wffn_out"], final_g,
                      layer=l, final_norm=(l == depth - 1))
    return x


def kernel(x_prompt, x_sample, norm1_g, w_in, hg_lb_logits, hg_norm_g, sg_norm_g, sg_norm_b,
           sg_w_spatial, sg_b_spatial, w_out, norm2_g, w_ffn_in, ffn_conv_w, ffn_conv_b,
           w_ffn_out, final_norm_g):
    depth = w_in.shape[0]
    w_in_b, w_out_b = _PerLayer(w_in.astype(BF16)), _PerLayer(w_out.astype(BF16))
    w_ffn_in_b = _PerLayer(w_ffn_in.astype(BF16))
    w_ffn_out_b = _PerLayer(w_ffn_out.astype(BF16).reshape(depth, FFN_STEPS, FFN_COLS, D_MODEL))
    layers = []
    for l in range(depth):
        layers.append(dict(
            g1=norm1_g[l].reshape(1, D_MODEL),
            w_in=w_in_b,
            logits_fwd=hg_lb_logits[0].astype(F32),
            logits_bwd=hg_lb_logits[1].astype(F32),
            hgn=hg_norm_g[l].reshape(1, HG_WIDTH),
            sgn_g=sg_norm_g[l].reshape(1, SG_WIDTH),
            sgn_b=sg_norm_b[l].reshape(1, SG_WIDTH),
            ws=sg_w_spatial[l],
            bs=sg_b_spatial[l].reshape(SG_HEADS, SG_CHUNK, 1),
            wout=w_out_b,
            g2=norm2_g[l].reshape(1, D_MODEL),
            win=w_ffn_in_b,
            cw=ffn_conv_w[l],
            cb=ffn_conv_b[l].reshape(1, 2 * D_FF),
            wffn_out=w_ffn_out_b,
        ))
    final_g = final_norm_g.reshape(1, D_MODEL)
    tables_fwd = _hgrn_tables(False)
    tables_bwd = _hgrn_tables(True)
    y_prompt = _trunk(x_prompt, layers, final_g, tables_fwd, tables_bwd)
    y_sample = _trunk(x_sample, layers, final_g, tables_fwd, tables_bwd)
    return (y_prompt, y_sample)
```
